```python
import jax, jax.numpy as jnp
from jax import lax
import numpy as np

D_MODEL = 2048
BATCH = 16
SEQ = 2048
DEPTH = 4
DEC_BATCH = 2
DEC_SEQ = 16384
PAST_LEN = 128

GROUP_CH = D_MODEL // 8
N_FOURIER_GROUPS = 4
POOL_WINDOWS = (2, 4, 8, 16)
N_POOL_GROUPS = len(POOL_WINDOWS)
A_WIDTH = N_FOURIER_GROUPS * GROUP_CH
B_WIDTH = N_POOL_GROUPS * GROUP_CH
MIX_WIDTH = A_WIDTH + B_WIDTH
CONV_WIDTH = 31
CONV_PAD = CONV_WIDTH // 2
CONV_INNER = D_MODEL
N_EXPERTS = 16
EXPERT_HIDDEN = D_MODEL
CAPACITY_FACTOR = 2
PLE_DIM = 256
N_EVEN = (DEPTH + 1) // 2
N_ODD = DEPTH // 2
EPS = 1e-6

kernel_name = "hybrid_fnet_pool_conformer_ec_encoder"


def rms_norm(x, g):
    xf = x.astype(jnp.float32)
    y = xf * lax.rsqrt(jnp.mean(xf * xf, axis=-1, keepdims=True) + EPS)
    return (y * g.astype(jnp.float32)).astype(x.dtype)


def layer_norm(x, g, b):
    xf = x.astype(jnp.float32)
    mu = jnp.mean(xf, axis=-1, keepdims=True)
    var = jnp.mean(jnp.square(xf - mu), axis=-1, keepdims=True)
    y = (xf - mu) * lax.rsqrt(var + EPS)
    return (y * g.astype(jnp.float32) + b.astype(jnp.float32)).astype(x.dtype)


def fourier_mix(u, w_a_mix):
    f = jnp.fft.fft2(u.astype(jnp.float32), axes=(1, 3), norm="ortho").real
    return jnp.einsum("bsgc,gcd->bsgd", f.astype(u.dtype), w_a_mix)


def pool_mix(u, w_b_mix, b_scale):
    bsz, s, g, c = u.shape
    uf = u.astype(jnp.float32)
    csum = jnp.concatenate([jnp.zeros((bsz, 1, g, c), jnp.float32), jnp.cumsum(uf, axis=1)], axis=1)
    t = jnp.arange(s)[:, None]
    w = jnp.array(POOL_WINDOWS, dtype=jnp.int32)[None, :]
    lo = jnp.clip(t - w // 2, 0, s)
    hi = jnp.clip(t - w // 2 + w, 0, s)
    gi = jnp.arange(g)[None, :]
    win_sum = csum[:, hi, gi, :] - csum[:, lo, gi, :]
    cnt = (hi - lo).astype(jnp.float32)[None, :, :, None]
    pooled = (win_sum / cnt - uf).astype(u.dtype)
    y = jnp.einsum("bsgc,gcd->bsgd", pooled, w_b_mix)
    return y * b_scale.reshape(g, c)


def fourier_pool_layer(hn, w_ab_in, w_a_mix, w_b_mix, b_scale, w_ab_out):
    bsz, s, _ = hn.shape
    u = hn @ w_ab_in
    ua = u[..., :A_WIDTH].reshape(bsz, s, N_FOURIER_GROUPS, GROUP_CH)
    ub = u[..., A_WIDTH:].reshape(bsz, s, N_POOL_GROUPS, GROUP_CH)
    ya = fourier_mix(ua, w_a_mix).reshape(bsz, s, A_WIDTH)
    yb = pool_mix(ub, w_b_mix, b_scale).reshape(bsz, s, B_WIDTH)
    return jnp.concatenate([ya, yb], axis=-1) @ w_ab_out


def conformer_conv(hn, w_c_in, c_dw, c_ln_g, c_ln_b, w_c_out):
    u = hn @ w_c_in
    a, b = jnp.split(u, 2, axis=-1)
    v = a * jax.nn.sigmoid(b)
    v = lax.conv_general_dilated(
        v, c_dw[:, None, :], window_strides=(1,), padding=[(CONV_PAD, CONV_PAD)],
        dimension_numbers=("NWC", "WIO", "NWC"), feature_group_count=CONV_INNER)
    v = layer_norm(v, c_ln_g, c_ln_b)
    v = jax.nn.silu(v)
    return v @ w_c_out


def expert_choice_ffn(hn, w_router, b_router, w_gate, w_up, w_down):
    bsz, s, d = hn.shape
    n = bsz * s
    xt = hn.reshape(n, d)
    logits = (xt @ w_router).astype(jnp.float32) + b_router.astype(jnp.float32)
    aff = jax.nn.softmax(logits, axis=-1)
    cap = CAPACITY_FACTOR * n // N_EXPERTS
    vals, idx = lax.top_k(aff.T, cap)
    xs = xt[idx]
    hid = jax.nn.silu(jnp.einsum("ecd,edf->ecf", xs, w_gate)) * jnp.einsum("ecd,edf->ecf", xs, w_up)
    ye = jnp.einsum("ecf,efd->ecd", hid, w_down) * vals[..., None].astype(hn.dtype)
    out = jnp.zeros((n, d), hn.dtype).at[idx.reshape(-1)].add(ye.reshape(-1, d))
    return out.reshape(bsz, s, d)


def run_trunk(x, p, g_mix, w_ab_in, w_a_mix, w_b_mix, b_scale, w_ab_out,
              w_c_in, c_dw, c_ln_g, c_ln_b, w_c_out, g_ffn, w_router, b_router,
              w_gate, w_up, w_down, g_ple, w_ple_gate, w_ple_proj, g_final):
    h = x
    for i in range(DEPTH):
        j = i // 2
        hn = rms_norm(h, g_mix[i])
        if i % 2 == 0:
            h = h + fourier_pool_layer(hn, w_ab_in[j], w_a_mix[j], w_b_mix[j], b_scale[j], w_ab_out[j])
        else:
            h = h + conformer_conv(hn, w_c_in[j], c_dw[j], c_ln_g[j], c_ln_b[j], w_c_out[j])
        h = h + expert_choice_ffn(rms_norm(h, g_ffn[i]), w_router[i], b_router[i], w_gate[i], w_up[i], w_down[i])
        gate = jax.nn.sigmoid(rms_norm(h, g_ple[i]) @ w_ple_gate[i])
        h = h + gate * (p[i] @ w_ple_proj[i])
    return rms_norm(h, g_final)


def setup_inputs(seed: int = 0) -> dict:
    key = jax.random.key(seed)
    ks = jax.random.split(key, 26)
    f32 = jnp.float32

    def nrm(k, shape, scale):
        return jax.random.normal(k, shape, f32) * scale

    def gain(k, shape):
        return 1.0 + 0.05 * jax.random.normal(k, shape, f32)

    return {
        "x_prompt": nrm(ks[0], (BATCH, SEQ, D_MODEL), 1.0),
        "x_sample": nrm(ks[1], (DEC_BATCH, DEC_SEQ, D_MODEL), 1.0),
        "p_prompt": nrm(ks[2], (DEPTH, BATCH, SEQ, PLE_DIM), 1.0),
        "p_sample": nrm(ks[3], (DEPTH, DEC_BATCH, DEC_SEQ, PLE_DIM), 1.0),
        "g_mix": gain(ks[4], (DEPTH, D_MODEL)),
        "w_ab_in": nrm(ks[5], (N_EVEN, D_MODEL, MIX_WIDTH), D_MODEL ** -0.5),
        "w_a_mix": nrm(ks[6], (N_EVEN, N_FOURIER_GROUPS, GROUP_CH, GROUP_CH), GROUP_CH ** -0.5),
        "w_b_mix": nrm(ks[7], (N_EVEN, N_POOL_GROUPS, GROUP_CH, GROUP_CH), GROUP_CH ** -0.5),
        "b_scale": 0.5 + 0.1 * jax.random.normal(ks[8], (N_EVEN, B_WIDTH), f32),
        "w_ab_out": nrm(ks[9], (N_EVEN, MIX_WIDTH, D_MODEL), MIX_WIDTH ** -0.5),
        "w_c_in": nrm(ks[10], (N_ODD, D_MODEL, 2 * CONV_INNER), D_MODEL ** -0.5),
        "c_dw": nrm(ks[11], (N_ODD, CONV_WIDTH, CONV_INNER), CONV_WIDTH ** -0.5),
        "c_ln_g": gain(ks[12], (N_ODD, CONV_INNER)),
        "c_ln_b": nrm(ks[13], (N_ODD, CONV_INNER), 0.02),
        "w_c_out": nrm(ks[14], (N_ODD, CONV_INNER, D_MODEL), CONV_INNER ** -0.5),
        "g_ffn": gain(ks[15], (DEPTH, D_MODEL)),
        "w_router": nrm(ks[16], (DEPTH, D_MODEL, N_EXPERTS), D_MODEL ** -0.5),
        "b_router": nrm(ks[17], (DEPTH, N_EXPERTS), 0.01),
        "w_gate": nrm(ks[18], (DEPTH, N_EXPERTS, D_MODEL, EXPERT_HIDDEN), D_MODEL ** -0.5),
        "w_up": nrm(ks[19], (DEPTH, N_EXPERTS, D_MODEL, EXPERT_HIDDEN), D_MODEL ** -0.5),
        "w_down": nrm(ks[20], (DEPTH, N_EXPERTS, EXPERT_HIDDEN, D_MODEL), EXPERT_HIDDEN ** -0.5),
        "g_ple": gain(ks[21], (DEPTH, D_MODEL)),
        "w_ple_gate": nrm(ks[22], (DEPTH, D_MODEL, D_MODEL), D_MODEL ** -0.5),
        "w_ple_proj": nrm(ks[23], (DEPTH, PLE_DIM, D_MODEL), PLE_DIM ** -0.5),
        "g_final": gain(ks[24], (D_MODEL,)),
    }


def reference(x_prompt, x_sample, p_prompt, p_sample, g_mix, w_ab_in, w_a_mix, w_b_mix, b_scale, w_ab_out,
              w_c_in, c_dw, c_ln_g, c_ln_b, w_c_out, g_ffn, w_router, b_router, w_gate, w_up, w_down,
              g_ple, w_ple_gate, w_ple_proj, g_final):
    y_prompt = run_trunk(x_prompt, p_prompt, g_mix, w_ab_in, w_a_mix, w_b_mix, b_scale, w_ab_out,
                         w_c_in, c_dw, c_ln_g, c_ln_b, w_c_out, g_ffn, w_router, b_router,
                         w_gate, w_up, w_down, g_ple, w_ple_gate, w_ple_proj, g_final)
    y_sample = run_trunk(x_sample, p_sample, g_mix, w_ab_in, w_a_mix, w_b_mix, b_scale, w_ab_out,
                         w_c_in, c_dw, c_ln_g, c_ln_b, w_c_out, g_ffn, w_router, b_router,
                         w_gate, w_up, w_down, g_ple, w_ple_gate, w_ple_proj, g_final)
    return (y_prompt, y_sample)
```

```python
import functools

import numpy as np
import jax
import jax.numpy as jnp
from jax import lax
from jax.experimental import pallas as pl
from jax.experimental.pallas import tpu as pltpu

f32 = jnp.float32
bf16 = jnp.bfloat16
i32 = jnp.int32

EPS = 1e-6
CAPACITY_FACTOR = 2
N_FOURIER_GROUPS = 4
POOL_WINDOWS = (2, 4, 8, 16)

LANES = 128
SUBLANES_F32 = 8
SUBLANES_BF16 = 16
MXU_DIM = 256
VMEM_LIMIT_BYTES = 56 * 1024 * 1024

ROUTE_CHUNK = MXU_DIM
DFT_INNER = 128
DFT_ROWS = 1024
HALO = 16


def _cp(*dims):
    return pltpu.CompilerParams(dimension_semantics=dims, vmem_limit_bytes=VMEM_LIMIT_BYTES)


def _rms(x, g):
    ms = jnp.mean(x * x, axis=-1, keepdims=True)
    return x * lax.rsqrt(ms + EPS) * g


def _sigmoid(x):
    return 1.0 / (1.0 + jnp.exp(-x))


def _dot(a, b):
    return jnp.dot(a, b, preferred_element_type=f32)


def _dot_nt(a, b):
    return lax.dot_general(a, b, (((1,), (1,)), ((), ())), preferred_element_type=f32)


def _even_in_kernel(h_ref, g_ref, w_ref, ua_ref, ub_ref):
    y = _rms(h_ref[...], g_ref[...]).astype(bf16)
    u = _dot(y, w_ref[...])
    half = ua_ref.shape[-1]
    ua_ref[...] = u[:, :half]
    ub_ref[...] = u[:, half:].astype(bf16)


def even_in(h, g, w, tm):
    n, d = h.shape
    mix = w.shape[1]
    half = mix // 2
    return pl.pallas_call(
        _even_in_kernel,
        grid=(n // tm,),
        in_specs=[
            pl.BlockSpec((tm, d), lambda i: (i, 0)),
            pl.BlockSpec((1, d), lambda i: (0, 0)),
            pl.BlockSpec((d, mix), lambda i: (0, 0)),
        ],
        out_specs=[
            pl.BlockSpec((tm, half), lambda i: (i, 0)),
            pl.BlockSpec((tm, half), lambda i: (i, 0)),
        ],
        out_shape=[jax.ShapeDtypeStruct((n, half), f32), jax.ShapeDtypeStruct((n, half), bf16)],
        compiler_params=_cp("parallel"),
        name="even_in",
    )(h, g, w)


def _dft_tables(seq, tb, ch):
    s2 = DFT_INNER
    s1 = seq // s2
    a = jnp.arange(s1, dtype=i32)
    ang1 = (2.0 * np.pi / s1) * ((a[:, None] * a[None, :]) % s1).astype(f32)
    f1 = jnp.concatenate([jnp.cos(ang1), -jnp.sin(ang1)], axis=0) * (1.0 / np.sqrt(s1))
    k1 = jnp.kron(f1, jnp.eye(tb, dtype=f32)).astype(bf16)
    b = jnp.arange(s2, dtype=i32)
    k = (b[None, None, :] * (b[None, :, None] * s1 + a[:, None, None])) % seq
    phi = (2.0 * np.pi / seq) * k.astype(f32)
    mr = jnp.cos(phi) * (1.0 / np.sqrt(s2))
    mi = -jnp.sin(phi) * (1.0 / np.sqrt(s2))
    t2 = jnp.concatenate(
        [jnp.concatenate([mr, -mi], axis=2), jnp.concatenate([mi, mr], axis=2)], axis=1
    ).astype(bf16)
    c = jnp.arange(ch, dtype=i32)
    th = (2.0 * np.pi / ch) * ((c[:, None] * c[None, :]) % ch).astype(f32)
    cs = (jnp.concatenate([jnp.cos(th), jnp.sin(th)], axis=0) * (1.0 / np.sqrt(ch))).astype(bf16)
    return k1, t2, cs


def _dft1_kernel(x_ref, k_ref, o_ref):
    _, s1, tb, cb = x_ref.shape
    x = x_ref[0].reshape(s1 * tb, cb).astype(bf16)
    r = _dot(k_ref[...], x)
    o_ref[0] = r.reshape(2, s1, tb, cb)


def dft_stage1(ua4, k1, tb, cb):
    bsz, s1, s2, c = ua4.shape
    return pl.pallas_call(
        _dft1_kernel,
        grid=(bsz, s2 // tb, c // cb),
        in_specs=[
            pl.BlockSpec((1, s1, tb, cb), lambda b, j, k: (b, 0, j, k)),
            pl.BlockSpec(k1.shape, lambda b, j, k: (0, 0)),
        ],
        out_specs=pl.BlockSpec((1, 2, s1, tb, cb), lambda b, j, k: (b, 0, 0, j, k)),
        out_shape=jax.ShapeDtypeStruct((bsz, 2, s1, s2, c), f32),
        compiler_params=_cp("parallel", "parallel", "parallel"),
        name="dft_stage1",
    )(ua4, k1)


def _dft2_kernel(a_ref, t_ref, cs_ref, w_ref, o_ref):
    s2, c = a_ref.shape[3], a_ref.shape[4]
    groups, ch, _ = w_ref.shape
    a = a_ref[0, :, 0].reshape(2 * s2, c).astype(bf16)
    p = _dot(t_ref[0], a)
    for g in range(groups):
        cols = slice(g * ch, (g + 1) * ch)
        pre = p[:s2, cols].astype(bf16)
        pim = p[s2:, cols].astype(bf16)
        f = _dot(pre, cs_ref[:ch, :]) + _dot(pim, cs_ref[ch:, :])
        y = _dot(f.astype(bf16), w_ref[g])
        o_ref[0, 0, :, cols] = y.astype(o_ref.dtype)


def dft_stage2(a5, t2, cs, w_a):
    bsz, _, s1, s2, c = a5.shape
    return pl.pallas_call(
        _dft2_kernel,
        grid=(bsz, s1),
        in_specs=[
            pl.BlockSpec((1, 2, 1, s2, c), lambda b, j: (b, 0, j, 0, 0)),
            pl.BlockSpec((1,) + t2.shape[1:], lambda b, j: (j, 0, 0)),
            pl.BlockSpec(cs.shape, lambda b, j: (0, 0)),
            pl.BlockSpec(w_a.shape, lambda b, j: (0, 0, 0)),
        ],
        out_specs=pl.BlockSpec((1, 1, s2, c), lambda b, j: (b, j, 0, 0)),
        out_shape=jax.ShapeDtypeStruct((bsz, s1, s2, c), bf16),
        compiler_params=_cp("parallel", "parallel"),
        name="dft_stage2",
    )(a5, t2, cs, w_a)


def _even_out_kernel(ub_ref, ubp_ref, ubn_ref, ya_ref, h_ref, wb_ref, bs_ref, wo_ref, o_ref, xs_ref,
                     *, seq):
    i = pl.program_id(1)
    nt = pl.num_programs(1)
    ts = ub_ref.shape[1]
    groups, ch, _ = wb_ref.shape
    aw = ya_ref.shape[2]
    xs_ref[0:HALO, :] = jnp.where(i > 0, ubp_ref[0].astype(f32), 0.0)
    xs_ref[HALO:HALO + ts, :] = ub_ref[0].astype(f32)
    xs_ref[HALO + ts:, :] = jnp.where(i < nt - 1, ubn_ref[0].astype(f32), 0.0)
    t = i * ts + lax.broadcasted_iota(i32, (ts, 1), 0)
    acc = h_ref[0] + _dot(ya_ref[0], wo_ref[0:aw, :])
    for g in range(groups):
        w = POOL_WINDOWS[g]
        cols = pl.ds(g * ch, ch)
        s = xs_ref[pl.ds(HALO - w // 2, ts), cols]
        for o in range(-w // 2 + 1, w // 2):
            s = s + xs_ref[pl.ds(HALO + o, ts), cols]
        cnt = (jnp.minimum(t + w // 2, seq) - jnp.maximum(t - w // 2, 0)).astype(f32)
        pooled = s / cnt - xs_ref[pl.ds(HALO, ts), cols]
        yb = _dot(pooled.astype(bf16), wb_ref[g]) * bs_ref[:, cols]
        acc = acc + _dot(yb.astype(bf16), wo_ref[pl.ds(aw + g * ch, ch), :])
    o_ref[0] = acc


def even_out(ub3, ya3, h3, w_b, b_scale, w_out, ts):
    bsz, seq, bw = ub3.shape
    d = h3.shape[2]
    aw = ya3.shape[2]
    nh = ts // HALO
    last = seq // HALO - 1
    return pl.pallas_call(
        functools.partial(_even_out_kernel, seq=seq),
        grid=(bsz, seq // ts),
        in_specs=[
            pl.BlockSpec((1, ts, bw), lambda b, i: (b, i, 0)),
            pl.BlockSpec((1, HALO, bw), lambda b, i: (b, jnp.maximum(i * nh - 1, 0), 0)),
            pl.BlockSpec((1, HALO, bw), lambda b, i: (b, jnp.minimum((i + 1) * nh, last), 0)),
            pl.BlockSpec((1, ts, aw), lambda b, i: (b, i, 0)),
            pl.BlockSpec((1, ts, d), lambda b, i: (b, i, 0)),
            pl.BlockSpec(w_b.shape, lambda b, i: (0, 0, 0)),
            pl.BlockSpec((1, bw), lambda b, i: (0, 0)),
            pl.BlockSpec(w_out.shape, lambda b, i: (0, 0)),
        ],
        out_specs=pl.BlockSpec((1, ts, d), lambda b, i: (b, i, 0)),
        out_shape=jax.ShapeDtypeStruct((bsz, seq, d), f32),
        scratch_shapes=[pltpu.VMEM((ts + 2 * HALO, bw), f32)],
        compiler_params=_cp("parallel", "parallel"),
        name="even_out",
    )(ub3, ub3, ub3, ya3, h3, w_b, b_scale, w_out)


def _odd_in_kernel(h_ref, g_ref, wa_ref, wb_ref, v_ref):
    y = _rms(h_ref[...], g_ref[...]).astype(bf16)
    a = _dot(y, wa_ref[...])
    b = _dot(y, wb_ref[...])
    v_ref[...] = a * _sigmoid(b)


def odd_in(h, g, w, tm, tn):
    n, d = h.shape
    inner = w.shape[1] // 2
    ncol = inner // tn
    return pl.pallas_call(
        _odd_in_kernel,
        grid=(ncol, n // tm),
        in_specs=[
            pl.BlockSpec((tm, d), lambda j, i: (i, 0)),
            pl.BlockSpec((1, d), lambda j, i: (0, 0)),
            pl.BlockSpec((d, tn), lambda j, i: (0, j)),
            pl.BlockSpec((d, tn), lambda j, i: (0, j + ncol)),
        ],
        out_specs=pl.BlockSpec((tm, tn), lambda j, i: (i, j)),
        out_shape=jax.ShapeDtypeStruct((n, inner), f32),
        compiler_params=_cp("parallel", "parallel"),
        name="odd_in",
    )(h, g, w, w)


CONV_ROWS = 32
CONV_COLS = 512


def _odd_out_kernel(v_ref, vp_ref, vn_ref, dw_ref, lg_ref, lb_ref, wo_ref, h_ref, o_ref, xs_ref, cv_ref,
                    *, taps):
    i = pl.program_id(1)
    nt = pl.num_programs(1)
    ts, inner = v_ref.shape[1], v_ref.shape[2]
    pad = taps // 2
    xs_ref[0:HALO, :] = jnp.where(i > 0, vp_ref[0], 0.0)
    xs_ref[HALO:HALO + ts, :] = v_ref[0]
    xs_ref[HALO + ts:, :] = jnp.where(i < nt - 1, vn_ref[0], 0.0)
    for c0 in range(0, inner, CONV_COLS):
        cols = pl.ds(c0, CONV_COLS)
        for r0 in range(0, ts, CONV_ROWS):
            acc = jnp.zeros((CONV_ROWS, CONV_COLS), f32)
            for k in range(taps):
                acc = acc + dw_ref[k:k + 1, cols] * xs_ref[pl.ds(r0 + HALO - pad + k, CONV_ROWS), cols]
            cv_ref[pl.ds(r0, CONV_ROWS), cols] = acc
    cv = cv_ref[...]
    mu = jnp.mean(cv, axis=-1, keepdims=True)
    xc = cv - mu
    var = jnp.mean(xc * xc, axis=-1, keepdims=True)
    y = xc * lax.rsqrt(var + EPS) * lg_ref[...] + lb_ref[...]
    y = y * _sigmoid(y)
    o_ref[0] = h_ref[0] + _dot(y.astype(bf16), wo_ref[...])


def odd_out(v3, dw, ln_g, ln_b, w_out, h3, ts):
    bsz, seq, inner = v3.shape
    d = h3.shape[2]
    taps = dw.shape[0]
    nh = ts // HALO
    last = seq // HALO - 1
    return pl.pallas_call(
        functools.partial(_odd_out_kernel, taps=taps),
        grid=(bsz, seq // ts),
        in_specs=[
            pl.BlockSpec((1, ts, inner), lambda b, i: (b, i, 0)),
            pl.BlockSpec((1, HALO, inner), lambda b, i: (b, jnp.maximum(i * nh - 1, 0), 0)),
            pl.BlockSpec((1, HALO, inner), lambda b, i: (b, jnp.minimum((i + 1) * nh, last), 0)),
            pl.BlockSpec(dw.shape, lambda b, i: (0, 0)),
            pl.BlockSpec((1, inner), lambda b, i: (0, 0)),
            pl.BlockSpec((1, inner), lambda b, i: (0, 0)),
            pl.BlockSpec(w_out.shape, lambda b, i: (0, 0)),
            pl.BlockSpec((1, ts, d), lambda b, i: (b, i, 0)),
        ],
        out_specs=pl.BlockSpec((1, ts, d), lambda b, i: (b, i, 0)),
        out_shape=jax.ShapeDtypeStruct((bsz, seq, d), f32),
        scratch_shapes=[pltpu.VMEM((ts + 2 * HALO, inner), f32), pltpu.VMEM((ts, inner), f32)],
        compiler_params=_cp("parallel", "parallel"),
        name="odd_out",
    )(v3, v3, v3, dw, ln_g, ln_b, w_out, h3)


def _router_kernel(h_ref, g_ref, wh_ref, wl_ref, b_ref, x_ref, at_ref, *, n_exp):
    tm, d = h_ref.shape
    xn = _rms(h_ref[...], g_ref[...])
    xh = xn.astype(bf16)
    xl = (xn - xh.astype(f32)).astype(bf16)
    logits = _dot(xh, wh_ref[...]) + _dot(xl, wh_ref[...]) + _dot(xh, wl_ref[...]) + b_ref[...]
    lane = lax.broadcasted_iota(i32, logits.shape, 1)
    logits = jnp.where(lane < n_exp, logits, -1e30)
    m = jnp.max(logits, axis=-1, keepdims=True)
    p = jnp.exp(logits - m)
    aff = p / jnp.sum(p, axis=-1, keepdims=True)
    x_ref[:, 0:d] = xn
    x_ref[:, d:] = aff
    at = aff.T
    for q in range(tm // ROUTE_CHUNK):
        at_ref[q] = at[0:n_exp, q * ROUTE_CHUNK:(q + 1) * ROUTE_CHUNK]


def router(h, g, wh, wl, b, n_exp, tm):
    n, d = h.shape
    return pl.pallas_call(
        functools.partial(_router_kernel, n_exp=n_exp),
        grid=(n // tm,),
        in_specs=[
            pl.BlockSpec((tm, d), lambda i: (i, 0)),
            pl.BlockSpec((1, d), lambda i: (0, 0)),
            pl.BlockSpec((d, LANES), lambda i: (0, 0)),
            pl.BlockSpec((d, LANES), lambda i: (0, 0)),
            pl.BlockSpec((1, LANES), lambda i: (0, 0)),
        ],
        out_specs=[
            pl.BlockSpec((tm, d + LANES), lambda i: (i, 0)),
            pl.BlockSpec((tm // ROUTE_CHUNK, n_exp, ROUTE_CHUNK), lambda i: (i, 0, 0)),
        ],
        out_shape=[
            jax.ShapeDtypeStruct((n, d + LANES), f32),
            jax.ShapeDtypeStruct((n // ROUTE_CHUNK, n_exp, ROUTE_CHUNK), f32),
        ],
        compiler_params=_cp("parallel"),
        name="router",
    )(h, g, wh, wl, b)


def _select_kernel(a_ref, su_ref, sl_ref, pos_ref, rank_ref, aux_ref, cst_ref, bits_ref, *, cap):
    nchunk, n_exp, cw = a_ref.shape
    bits_ref[...] = pltpu.bitcast(a_ref[...], i32)

    def count_ge(cand):
        m = (bits_ref[...] >= cand[None]).astype(f32)
        return jnp.sum(jnp.sum(m, axis=0), axis=1, keepdims=True)

    def bisect(it, prefix):
        cand = prefix | jnp.left_shift(jnp.int32(1), 30 - it)
        return jnp.where(count_ge(cand) >= cap, cand, prefix)

    thr = lax.fori_loop(0, 31, bisect, jnp.zeros((n_exp, 1), i32))
    need = cap - (count_ge(thr + 1))

    sub = lax.broadcasted_iota(i32, (SUBLANES_F32, cw), 0)

    def chunk(c, carry):
        cg, cq = carry
        b = bits_ref[c]
        g = b > thr
        q = b == thr
        gq = jnp.concatenate([g.astype(f32), q.astype(f32)], axis=0).astype(bf16)
        pre = _dot(gq, su_ref[...])
        pg = cg + pre[:n_exp]
        pq = cq + pre[n_exp:]
        sel = g | (q & (pq < need))
        pos = pg + jnp.minimum(pq, need)
        self32 = sel.astype(f32)
        pos_ref[c] = jnp.where(sel, pos, -1.0).astype(i32)
        rtok = jnp.sum(pos, axis=0, keepdims=True)
        ktok = jnp.sum(self32, axis=0, keepdims=True)
        er = _dot(sl_ref[...], self32.astype(bf16))
        rank_ref[c] = (rtok + er).astype(i32)
        cstart = cg + jnp.minimum(cq, need)
        rstart = jnp.sum(cstart, axis=0, keepdims=True)
        aux = jnp.where(sub == 0, rtok, jnp.where(sub == 1, ktok, jnp.where(sub == 2, rstart, 0.0)))
        aux_ref[c] = aux.astype(i32)
        cst_ref[c] = jnp.broadcast_to(cstart, (n_exp, LANES)).astype(i32)
        return (cg + jnp.sum(g.astype(f32), axis=1, keepdims=True),
                cq + jnp.sum(q.astype(f32), axis=1, keepdims=True))

    zero = jnp.zeros((n_exp, 1), f32)
    lax.fori_loop(0, nchunk, chunk, (zero, zero))


def select(aff3, cap):
    nchunk, n_exp, cw = aff3.shape
    r = lax.broadcasted_iota(i32, (cw, cw), 0)
    c = lax.broadcasted_iota(i32, (cw, cw), 1)
    su = (r < c).astype(bf16)
    re = lax.broadcasted_iota(i32, (n_exp, n_exp), 0)
    ce = lax.broadcasted_iota(i32, (n_exp, n_exp), 1)
    sl = (ce < re).astype(bf16)
    full3 = lambda shape: pl.BlockSpec(shape, lambda i: (0, 0, 0))
    return pl.pallas_call(
        functools.partial(_select_kernel, cap=cap),
        grid=(1,),
        in_specs=[full3(aff3.shape), pl.BlockSpec(su.shape, lambda i: (0, 0)),
                  pl.BlockSpec(sl.shape, lambda i: (0, 0))],
        out_specs=[full3((nchunk, n_exp, cw)), full3((nchunk, n_exp, cw)),
                   full3((nchunk, SUBLANES_F32, cw)), full3((nchunk, n_exp, LANES))],
        out_shape=[
            jax.ShapeDtypeStruct((nchunk, n_exp, cw), i32),
            jax.ShapeDtypeStruct((nchunk, n_exp, cw), i32),
            jax.ShapeDtypeStruct((nchunk, SUBLANES_F32, cw), i32),
            jax.ShapeDtypeStruct((nchunk, n_exp, LANES), i32),
        ],
        scratch_shapes=[pltpu.VMEM((nchunk, n_exp, cw), i32)],
        compiler_params=_cp("arbitrary"),
        name="moe_select",
    )(aff3, su, sl)


def _compact_kernel(cst_ref, pos_ref, rank_ref, idx_ref, dst_ref, acc_ref):
    e = pl.program_id(0)
    nchunk, _, cw = pos_ref.shape
    nblk = idx_ref.shape[1]
    acc_ref[...] = jnp.zeros(acc_ref.shape, f32)
    sub = lax.broadcasted_iota(i32, (SUBLANES_BF16, cw), 0)
    slot = lax.broadcasted_iota(i32, (cw, cw), 0)
    lane = lax.broadcasted_iota(i32, (1, cw), 1)

    def chunk(c, _):
        b0 = cst_ref[e * nchunk + c] // cw
        p = pos_ref[c, pl.ds(e, 1), :]
        rk = rank_ref[c, pl.ds(e, 1), :]
        tok = c * cw + lane
        vals = jnp.where(sub == 0, tok >> 8, jnp.where(sub == 1, tok & 255,
               jnp.where(sub == 2, rk >> 8, jnp.where(sub == 3, rk & 255, 0))))
        vt = vals.astype(f32).astype(bf16)
        rel = p - b0 * cw
        for blk in range(2):
            onehot = (rel - blk * cw == slot).astype(f32).astype(bf16)
            acc_ref[b0 + blk] += _dot_nt(vt, onehot)
        return 0

    lax.fori_loop(0, nchunk, chunk, 0)
    for b in range(nblk):
        a = acc_ref[b]
        idx_ref[0, b:b + 1, :] = (a[0:1, :] * 256.0 + a[1:2, :]).astype(i32)
        dst_ref[0, b:b + 1, :] = (a[2:3, :] * 256.0 + a[3:4, :]).astype(i32)


def compact(cst_flat, pos3, rank3, cap):
    nchunk, n_exp, cw = pos3.shape
    nblk = cap // cw
    grid_spec = pltpu.PrefetchScalarGridSpec(
        num_scalar_prefetch=1,
        grid=(n_exp,),
        in_specs=[pl.BlockSpec(pos3.shape, lambda e, s: (0, 0, 0)),
                  pl.BlockSpec(rank3.shape, lambda e, s: (0, 0, 0))],
        out_specs=[pl.BlockSpec((1, nblk, cw), lambda e, s: (e, 0, 0)),
                   pl.BlockSpec((1, nblk, cw), lambda e, s: (e, 0, 0))],
        scratch_shapes=[pltpu.VMEM((nblk + 1, SUBLANES_BF16, cw), f32)],
    )
    return pl.pallas_call(
        _compact_kernel,
        grid_spec=grid_spec,
        out_shape=[jax.ShapeDtypeStruct((n_exp, nblk, cw), i32),
                   jax.ShapeDtypeStruct((n_exp, nblk, cw), i32)],
        compiler_params=_cp("arbitrary"),
        name="moe_compact",
    )(cst_flat, pos3, rank3)


def _expert_kernel(idx_ref, dst_ref, x_hbm, wg_ref, wu_ref, wd_ref, z_hbm,
                   xbuf, xb16, vcol, acc, ybuf, gsem, ssem, *, cap):
    e = pl.program_id(0)
    m = pl.program_id(1)
    f = pl.program_id(2)
    nm = pl.num_programs(1)
    nf = pl.num_programs(2)
    bm, d = acc.shape
    base = e * cap + m * bm

    def gather_row(j):
        return pltpu.make_async_copy(x_hbm.at[pl.ds(idx_ref[base + j], 1), :], xbuf.at[pl.ds(j, 1), :], gsem)

    def scatter_row(j, row_base):
        return pltpu.make_async_copy(ybuf.at[pl.ds(j, 1), :], z_hbm.at[pl.ds(dst_ref[row_base + j], 1), :], ssem)

    def wait_scatters(row_base):
        def body(j, _):
            scatter_row(j, row_base).wait()
            return 0
        lax.fori_loop(0, bm, body, 0)

    @pl.when(f == 0)
    def _():
        def start(j, _):
            gather_row(j).start()
            return 0
        lax.fori_loop(0, bm, start, 0)

        def wait(j, _):
            gather_row(j).wait()
            return 0
        lax.fori_loop(0, bm, wait, 0)
        xb16[...] = xbuf[:, 0:d].astype(bf16)
        lane = lax.broadcasted_iota(i32, (bm, LANES), 1)
        vcol[...] = jnp.sum(jnp.where(lane == e, xbuf[:, d:], 0.0), axis=1, keepdims=True)

    x = xb16[...]
    gate = _dot(x, wg_ref[0])
    up = _dot(x, wu_ref[0])
    hid = (gate * _sigmoid(gate) * up).astype(bf16)
    part = _dot(hid, wd_ref[0])

    @pl.when(f == 0)
    def _():
        acc[...] = part

    @pl.when(f > 0)
    def _():
        acc[...] += part

    @pl.when(f == nf - 1)
    def _():
        @pl.when(e * nm + m > 0)
        def _():
            wait_scatters(base - bm)

        ybuf[...] = acc[...] * vcol[...]

        def start(j, _):
            scatter_row(j, base).start()
            return 0
        lax.fori_loop(0, bm, start, 0)

        @pl.when((e == pl.num_programs(0) - 1) & (m == nm - 1))
        def _():
            wait_scatters(base)


def experts(idx, dst, xext, wg, wu, wd, cap, bm, tf):
    n_exp, d, hidden = wg.shape
    grid_spec = pltpu.PrefetchScalarGridSpec(
        num_scalar_prefetch=2,
        grid=(n_exp, cap // bm, hidden // tf),
        in_specs=[
            pl.BlockSpec(memory_space=pl.ANY),
            pl.BlockSpec((1, d, tf), lambda e, m, f, i_, d_: (e, 0, f)),
            pl.BlockSpec((1, d, tf), lambda e, m, f, i_, d_: (e, 0, f)),
            pl.BlockSpec((1, tf, d), lambda e, m, f, i_, d_: (e, f, 0)),
        ],
        out_specs=pl.BlockSpec(memory_space=pl.ANY),
        scratch_shapes=[
            pltpu.VMEM((bm, d + LANES), f32),
            pltpu.VMEM((bm, d), bf16),
            pltpu.VMEM((bm, 1), f32),
            pltpu.VMEM((bm, d), f32),
            pltpu.VMEM((bm, d), f32),
            pltpu.SemaphoreType.DMA,
            pltpu.SemaphoreType.DMA,
        ],
    )
    return pl.pallas_call(
        functools.partial(_expert_kernel, cap=cap),
        grid_spec=grid_spec,
        out_shape=jax.ShapeDtypeStruct((n_exp * cap, d), f32),
        compiler_params=_cp("arbitrary", "arbitrary", "arbitrary"),
        name="moe_experts",
    )(idx, dst, xext, wg, wu, wd)


def _combine_kernel(ts_ref, h_ref, aux_ref, eye_ref, z_hbm, o_ref, zbuf, sem, *, total):
    i = pl.program_id(0)
    tm, d = h_ref.shape
    cw = zbuf.shape[0]
    sub = lax.broadcasted_iota(i32, (SUBLANES_BF16, cw), 0)
    rt = aux_ref[0, 0:1, :]
    kt = aux_ref[0, 1:2, :]
    parts = jnp.where(sub == 0, rt >> 8, jnp.where(sub == 1, rt & 255, jnp.where(sub == 2, kt, 0)))
    col = _dot_nt(eye_ref[...], parts.astype(f32).astype(bf16))
    rlo = col[:, 0:1] * 256.0 + col[:, 1:2]
    rhi = rlo + col[:, 2:3]
    r_lo = ts_ref[i]
    r_hi = ts_ref[i + 1]
    a0 = (r_lo // SUBLANES_F32) * SUBLANES_F32
    nch = (r_hi - a0 + cw - 1) // cw
    o_ref[...] = h_ref[...]
    lane = lax.broadcasted_iota(i32, (1, cw), 1)

    def chunk(k, _):
        nominal = a0 + k * cw
        start = pl.multiple_of(jnp.minimum(nominal, total - cw), SUBLANES_F32)
        cp = pltpu.make_async_copy(z_hbm.at[pl.ds(start, cw), :], zbuf, sem)
        cp.start()
        cp.wait()
        row = start + lane
        rowf = row.astype(f32)
        onehot = ((rowf >= rlo) & (rowf < rhi) & (row >= nominal)).astype(f32).astype(bf16)
        o_ref[...] += _dot(onehot, zbuf[...].astype(bf16))
        return 0

    lax.fori_loop(0, nch, chunk, 0)


def combine(tstart, h, aux3, z):
    n, d = h.shape
    cw = aux3.shape[2]
    total = z.shape[0]
    eye = jnp.eye(cw, dtype=bf16)
    grid_spec = pltpu.PrefetchScalarGridSpec(
        num_scalar_prefetch=1,
        grid=(n // cw,),
        in_specs=[
            pl.BlockSpec((cw, d), lambda i, s: (i, 0)),
            pl.BlockSpec((1,) + aux3.shape[1:], lambda i, s: (i, 0, 0)),
            pl.BlockSpec(eye.shape, lambda i, s: (0, 0)),
            pl.BlockSpec(memory_space=pl.ANY),
        ],
        out_specs=pl.BlockSpec((cw, d), lambda i, s: (i, 0)),
        scratch_shapes=[pltpu.VMEM((cw, d), f32), pltpu.SemaphoreType.DMA],
    )
    return pl.pallas_call(
        functools.partial(_combine_kernel, total=total),
        grid_spec=grid_spec,
        out_shape=jax.ShapeDtypeStruct((n, d), f32),
        compiler_params=_cp("arbitrary"),
        name="moe_combine",
    )(tstart, h, aux3, eye, z)


def moe(h, g, wr_hi, wr_lo, b_r, wg, wu, wd, n_exp, tm, bm, tf):
    n, d = h.shape
    cap = CAPACITY_FACTOR * n // n_exp
    xext, aff3 = router(h, g, wr_hi, wr_lo, b_r, n_exp, tm)
    pos3, rank3, aux3, cst3 = select(aff3, cap)
    cst_flat = cst3[:, :, 0].T.reshape(-1)
    idx, dst = compact(cst_flat, pos3, rank3, cap)
    z = experts(idx.reshape(-1), dst.reshape(-1), xext, wg, wu, wd, cap, bm, tf)
    tstart = jnp.concatenate([aux3[:, 2, 0], jnp.full((1,), n_exp * cap, i32)])
    return combine(tstart, h, aux3, z)


def _ple_kernel(h_ref, p_ref, g_ref, wg_ref, wp_ref, gf_ref, o_ref, *, final):
    h = h_ref[...]
    gate = _sigmoid(_dot(_rms(h, g_ref[...]).astype(bf16), wg_ref[...]))
    out = h + gate * _dot(p_ref[...].astype(bf16), wp_ref[...])
    if final:
        out = _rms(out, gf_ref[...])
    o_ref[...] = out


def ple(h, p, g, w_gate, w_proj, g_final, final, tm):
    n, d = h.shape
    pd = p.shape[1]
    return pl.pallas_call(
        functools.partial(_ple_kernel, final=final),
        grid=(n // tm,),
        in_specs=[
            pl.BlockSpec((tm, d), lambda i: (i, 0)),
            pl.BlockSpec((tm, pd), lambda i: (i, 0)),
            pl.BlockSpec((1, d), lambda i: (0, 0)),
            pl.BlockSpec((d, d), lambda i: (0, 0)),
            pl.BlockSpec((pd, d), lambda i: (0, 0)),
            pl.BlockSpec((1, d), lambda i: (0, 0)),
        ],
        out_specs=pl.BlockSpec((tm, d), lambda i: (i, 0)),
        out_shape=jax.ShapeDtypeStruct((n, d), f32),
        compiler_params=_cp("parallel"),
        name="ple_final" if final else "ple",
    )(h, p, g, w_gate, w_proj, g_final)


def _tiles(n, seq, cap):
    tm = min(512, n)
    ts = min(256, seq)
    bm = min(512, cap)
    return tm, ts, bm


def _run_trunk(x, p, wts, n_exp):
    bsz, seq, d = x.shape
    depth = p.shape[0]
    n = bsz * seq
    cap = CAPACITY_FACTOR * n // n_exp
    tm, ts, bm = _tiles(n, seq, cap)
    hidden = wts["w_gate"].shape[-1]
    tf = min(512, hidden)
    inner = wts["w_c_out"].shape[1]
    tn = min(1024, inner)
    aw = wts["w_a_mix"].shape[1] * wts["w_a_mix"].shape[2]
    ch = wts["w_a_mix"].shape[2]
    s2 = DFT_INNER
    s1 = seq // s2
    tb = min(s2, max(SUBLANES_F32, DFT_ROWS // s1))
    k1, t2, cs = _dft_tables(seq, tb, ch)
    cb = min(512, aw)

    h = x.reshape(n, d)
    for i in range(depth):
        j = i // 2
        if i % 2 == 0:
            ua, ub = even_in(h, wts["g_mix"][i], wts["w_ab_in"][j], tm)
            a5 = dft_stage1(ua.reshape(bsz, s1, s2, aw), k1, tb, cb)
            yp = dft_stage2(a5, t2, cs, wts["w_a_mix"][j])
            ya = jnp.swapaxes(yp, 1, 2).reshape(bsz, seq, aw)
            h = even_out(ub.reshape(bsz, seq, -1), ya, h.reshape(bsz, seq, d), wts["w_b_mix"][j],
                         wts["b_scale"][j], wts["w_ab_out"][j], ts).reshape(n, d)
        else:
            v = odd_in(h, wts["g_mix"][i], wts["w_c_in"][j], tm, tn)
            h = odd_out(v.reshape(bsz, seq, inner), wts["c_dw"][j], wts["c_ln_g"][j], wts["c_ln_b"][j],
                        wts["w_c_out"][j], h.reshape(bsz, seq, d), ts).reshape(n, d)
        h = moe(h, wts["g_ffn"][i], wts["w_router_hi"][i], wts["w_router_lo"][i], wts["b_router"][i],
                wts["w_gate"][i], wts["w_up"][i], wts["w_down"][i], n_exp, tm, bm, tf)
        h = ple(h, p[i].reshape(n, -1), wts["g_ple"][i], wts["w_ple_gate"][i], wts["w_ple_proj"][i],
                wts["g_final"], i == depth - 1, tm)
    return h.reshape(bsz, seq, d)


def kernel(x_prompt, x_sample, p_prompt, p_sample, g_mix, w_ab_in, w_a_mix, w_b_mix, b_scale, w_ab_out,
           w_c_in, c_dw, c_ln_g, c_ln_b, w_c_out, g_ffn, w_router, b_router, w_gate, w_up, w_down,
           g_ple, w_ple_gate, w_ple_proj, g_final):
    depth, d = g_mix.shape
    n_exp = w_router.shape[-1]
    wr = jnp.pad(w_router, ((0, 0), (0, 0), (0, LANES - n_exp)))
    wr_hi = wr.astype(bf16)
    wr_lo = (wr - wr_hi.astype(f32)).astype(bf16)
    wts = dict(
        g_mix=g_mix[:, None, :],
        w_ab_in=w_ab_in.astype(bf16),
        w_a_mix=w_a_mix.astype(bf16),
        w_b_mix=w_b_mix.astype(bf16),
        b_scale=b_scale[:, None, :],
        w_ab_out=w_ab_out.astype(bf16),
        w_c_in=w_c_in.astype(bf16),
        c_dw=c_dw,
        c_ln_g=c_ln_g[:, None, :],
        c_ln_b=c_ln_b[:, None, :],
        w_c_out=w_c_out.astype(bf16),
        g_ffn=g_ffn[:, None, :],
        w_router_hi=wr_hi,
        w_router_lo=wr_lo,
        b_router=jnp.pad(b_router, ((0, 0), (0, LANES - n_exp)))[:, None, :],
        w_gate=w_gate.astype(bf16),
        w_up=w_up.astype(bf16),
        w_down=w_down.astype(bf16),
        g_ple=g_ple[:, None, :],
        w_ple_gate=w_ple_gate.astype(bf16),
        w_ple_proj=w_ple_proj.astype(bf16),
        g_final=g_final[None, :],
    )
    y_prompt = _run_trunk(x_prompt, p_prompt, wts, n_exp)
    y_sample = _run_trunk(x_sample, p_sample, wts, n_exp)
    return (y_prompt, y_sample)
```

```python
import functools

import numpy as np
import jax
import jax.numpy as jnp
from jax import lax
from jax.experimental import pallas as pl
from jax.experimental.pallas import tpu as pltpu

f32 = jnp.float32
bf16 = jnp.bfloat16
i32 = jnp.int32

EPS = 1e-6
CAPACITY_FACTOR = 2
N_FOURIER_GROUPS = 4
POOL_WINDOWS = (2, 4, 8, 16)

LANES = 128
SUBLANES_F32 = 8
SUBLANES_BF16 = 16
MXU_DIM = 256
VMEM_LIMIT_BYTES = 56 * 1024 * 1024

ROUTE_CHUNK = MXU_DIM
DFT_INNER = 128
DFT_ROWS = 1024
HALO = 16
ISSUE_UNROLL = 8


def _cp(*dims):
    return pltpu.CompilerParams(dimension_semantics=dims, vmem_limit_bytes=VMEM_LIMIT_BYTES)


def _rms(x, g):
    ms = jnp.mean(x * x, axis=-1, keepdims=True)
    return x * lax.rsqrt(ms + EPS) * g


def _sigmoid(x):
    return 1.0 / (1.0 + jnp.exp(-x))


def _dot(a, b):
    return jnp.dot(a, b, preferred_element_type=f32)


def _dot_nt(a, b):
    return lax.dot_general(a, b, (((1,), (1,)), ((), ())), preferred_element_type=f32)


def _even_in_kernel(h_ref, g_ref, w_ref, ua_ref, ub_ref):
    y = _rms(h_ref[...], g_ref[...]).astype(bf16)
    u = _dot(y, w_ref[...])
    half = ua_ref.shape[-1]
    ua_ref[...] = u[:, :half]
    ub_ref[...] = u[:, half:].astype(bf16)


def even_in(h, g, w, tm):
    n, d = h.shape
    mix = w.shape[1]
    half = mix // 2
    return pl.pallas_call(
        _even_in_kernel,
        grid=(n // tm,),
        in_specs=[
            pl.BlockSpec((tm, d), lambda i: (i, 0)),
            pl.BlockSpec((1, d), lambda i: (0, 0)),
            pl.BlockSpec((d, mix), lambda i: (0, 0)),
        ],
        out_specs=[
            pl.BlockSpec((tm, half), lambda i: (i, 0)),
            pl.BlockSpec((tm, half), lambda i: (i, 0)),
        ],
        out_shape=[jax.ShapeDtypeStruct((n, half), f32), jax.ShapeDtypeStruct((n, half), bf16)],
        compiler_params=_cp("parallel"),
        name="even_in",
    )(h, g, w)


def _dft_tables(seq, tb, ch):
    s2 = DFT_INNER
    s1 = seq // s2
    a = jnp.arange(s1, dtype=i32)
    ang1 = (2.0 * np.pi / s1) * ((a[:, None] * a[None, :]) % s1).astype(f32)
    f1 = jnp.concatenate([jnp.cos(ang1), -jnp.sin(ang1)], axis=0) * (1.0 / np.sqrt(s1))
    k1 = jnp.kron(f1, jnp.eye(tb, dtype=f32)).astype(bf16)
    b = jnp.arange(s2, dtype=i32)
    k = (b[None, None, :] * (b[None, :, None] * s1 + a[:, None, None])) % seq
    phi = (2.0 * np.pi / seq) * k.astype(f32)
    mr = jnp.cos(phi) * (1.0 / np.sqrt(s2))
    mi = -jnp.sin(phi) * (1.0 / np.sqrt(s2))
    t2 = jnp.concatenate(
        [jnp.concatenate([mr, -mi], axis=2), jnp.concatenate([mi, mr], axis=2)], axis=1
    ).astype(bf16)
    c = jnp.arange(ch, dtype=i32)
    th = (2.0 * np.pi / ch) * ((c[:, None] * c[None, :]) % ch).astype(f32)
    cs = (jnp.concatenate([jnp.cos(th), jnp.sin(th)], axis=0) * (1.0 / np.sqrt(ch))).astype(bf16)
    return k1, t2, cs


def _dft1_kernel(x_ref, k_ref, o_ref):
    _, s1, tb, cb = x_ref.shape
    x = x_ref[0].reshape(s1 * tb, cb).astype(bf16)
    r = _dot(k_ref[...], x)
    o_ref[0] = r.reshape(2, s1, tb, cb)


def dft_stage1(ua4, k1, tb, cb):
    bsz, s1, s2, c = ua4.shape
    return pl.pallas_call(
        _dft1_kernel,
        grid=(bsz, s2 // tb, c // cb),
        in_specs=[
            pl.BlockSpec((1, s1, tb, cb), lambda b, j, k: (b, 0, j, k)),
            pl.BlockSpec(k1.shape, lambda b, j, k: (0, 0)),
        ],
        out_specs=pl.BlockSpec((1, 2, s1, tb, cb), lambda b, j, k: (b, 0, 0, j, k)),
        out_shape=jax.ShapeDtypeStruct((bsz, 2, s1, s2, c), f32),
        compiler_params=_cp("parallel", "parallel", "parallel"),
        name="dft_stage1",
    )(ua4, k1)


def _dft2_kernel(a_ref, t_ref, cs_ref, w_ref, o_ref):
    s2, c = a_ref.shape[3], a_ref.shape[4]
    groups, ch, _ = w_ref.shape
    a = a_ref[0, :, 0].reshape(2 * s2, c).astype(bf16)
    p = _dot(t_ref[0], a)
    for g in range(groups):
        cols = slice(g * ch, (g + 1) * ch)
        pre = p[:s2, cols].astype(bf16)
        pim = p[s2:, cols].astype(bf16)
        f = _dot(pre, cs_ref[:ch, :]) + _dot(pim, cs_ref[ch:, :])
        y = _dot(f.astype(bf16), w_ref[g])
        o_ref[0, 0, :, cols] = y.astype(o_ref.dtype)


def dft_stage2(a5, t2, cs, w_a):
    bsz, _, s1, s2, c = a5.shape
    return pl.pallas_call(
        _dft2_kernel,
        grid=(bsz, s1),
        in_specs=[
            pl.BlockSpec((1, 2, 1, s2, c), lambda b, j: (b, 0, j, 0, 0)),
            pl.BlockSpec((1,) + t2.shape[1:], lambda b, j: (j, 0, 0)),
            pl.BlockSpec(cs.shape, lambda b, j: (0, 0)),
            pl.BlockSpec(w_a.shape, lambda b, j: (0, 0, 0)),
        ],
        out_specs=pl.BlockSpec((1, 1, s2, c), lambda b, j: (b, j, 0, 0)),
        out_shape=jax.ShapeDtypeStruct((bsz, s1, s2, c), bf16),
        compiler_params=_cp("parallel", "parallel"),
        name="dft_stage2",
    )(a5, t2, cs, w_a)


def _even_out_kernel(ub_ref, ubp_ref, ubn_ref, ya_ref, h_ref, wb_ref, bs_ref, wo_ref, o_ref, xs_ref,
                     *, seq):
    i = pl.program_id(1)
    nt = pl.num_programs(1)
    ts = ub_ref.shape[1]
    groups, ch, _ = wb_ref.shape
    aw = ya_ref.shape[2]
    xs_ref[0:HALO, :] = jnp.where(i > 0, ubp_ref[0].astype(f32), 0.0)
    xs_ref[HALO:HALO + ts, :] = ub_ref[0].astype(f32)
    xs_ref[HALO + ts:, :] = jnp.where(i < nt - 1, ubn_ref[0].astype(f32), 0.0)
    t = i * ts + lax.broadcasted_iota(i32, (ts, 1), 0)
    acc = h_ref[0] + _dot(ya_ref[0], wo_ref[0:aw, :])
    for g in range(groups):
        w = POOL_WINDOWS[g]
        cols = pl.ds(g * ch, ch)
        s = xs_ref[pl.ds(HALO - w // 2, ts), cols]
        for o in range(-w // 2 + 1, w // 2):
            s = s + xs_ref[pl.ds(HALO + o, ts), cols]
        cnt = (jnp.minimum(t + w // 2, seq) - jnp.maximum(t - w // 2, 0)).astype(f32)
        pooled = s / cnt - xs_ref[pl.ds(HALO, ts), cols]
        yb = _dot(pooled.astype(bf16), wb_ref[g]) * bs_ref[:, cols]
        acc = acc + _dot(yb.astype(bf16), wo_ref[pl.ds(aw + g * ch, ch), :])
    o_ref[0] = acc


def even_out(ub3, ya3, h3, w_b, b_scale, w_out, ts):
    bsz, seq, bw = ub3.shape
    d = h3.shape[2]
    aw = ya3.shape[2]
    nh = ts // HALO
    last = seq // HALO - 1
    return pl.pallas_call(
        functools.partial(_even_out_kernel, seq=seq),
        grid=(bsz, seq // ts),
        in_specs=[
            pl.BlockSpec((1, ts, bw), lambda b, i: (b, i, 0)),
            pl.BlockSpec((1, HALO, bw), lambda b, i: (b, jnp.maximum(i * nh - 1, 0), 0)),
            pl.BlockSpec((1, HALO, bw), lambda b, i: (b, jnp.minimum((i + 1) * nh, last), 0)),
            pl.BlockSpec((1, ts, aw), lambda b, i: (b, i, 0)),
            pl.BlockSpec((1, ts, d), lambda b, i: (b, i, 0)),
            pl.BlockSpec(w_b.shape, lambda b, i: (0, 0, 0)),
            pl.BlockSpec((1, bw), lambda b, i: (0, 0)),
            pl.BlockSpec(w_out.shape, lambda b, i: (0, 0)),
        ],
        out_specs=pl.BlockSpec((1, ts, d), lambda b, i: (b, i, 0)),
        out_shape=jax.ShapeDtypeStruct((bsz, seq, d), f32),
        scratch_shapes=[pltpu.VMEM((ts + 2 * HALO, bw), f32)],
        compiler_params=_cp("parallel", "parallel"),
        name="even_out",
    )(ub3, ub3, ub3, ya3, h3, w_b, b_scale, w_out)


def _odd_in_kernel(h_ref, g_ref, wa_ref, wb_ref, v_ref):
    y = _rms(h_ref[...], g_ref[...]).astype(bf16)
    a = _dot(y, wa_ref[...])
    b = _dot(y, wb_ref[...])
    v_ref[...] = a * _sigmoid(b)


def odd_in(h, g, w, tm, tn):
    n, d = h.shape
    inner = w.shape[1] // 2
    ncol = inner // tn
    return pl.pallas_call(
        _odd_in_kernel,
        grid=(ncol, n // tm),
        in_specs=[
            pl.BlockSpec((tm, d), lambda j, i: (i, 0)),
            pl.BlockSpec((1, d), lambda j, i: (0, 0)),
            pl.BlockSpec((d, tn), lambda j, i: (0, j)),
            pl.BlockSpec((d, tn), lambda j, i: (0, j + ncol)),
        ],
        out_specs=pl.BlockSpec((tm, tn), lambda j, i: (i, j)),
        out_shape=jax.ShapeDtypeStruct((n, inner), f32),
        compiler_params=_cp("parallel", "parallel"),
        name="odd_in",
    )(h, g, w, w)


CONV_ROWS = 32
CONV_COLS = 512


def _odd_out_kernel(v_ref, vp_ref, vn_ref, dw_ref, lg_ref, lb_ref, wo_ref, h_ref, o_ref, xs_ref, sh_ref,
                    cv_ref, *, taps):
    i = pl.program_id(1)
    nt = pl.num_programs(1)
    ts, inner = v_ref.shape[1], v_ref.shape[2]
    first = HALO - taps // 2
    span = sh_ref.shape[1]
    xs_ref[0:HALO, :] = jnp.where(i > 0, vp_ref[0], 0.0)
    xs_ref[HALO:HALO + ts, :] = v_ref[0]
    xs_ref[HALO + ts:, :] = jnp.where(i < nt - 1, vn_ref[0], 0.0)
    for c0 in range(0, inner, CONV_COLS):
        cols = pl.ds(c0, CONV_COLS)
        for r in range(1, SUBLANES_F32):
            sh_ref[r] = xs_ref[pl.ds(r, span), cols]
        for r0 in range(0, ts, CONV_ROWS):
            acc = jnp.zeros((CONV_ROWS, CONV_COLS), f32)
            for k in range(taps):
                q, r = divmod(first + k, SUBLANES_F32)
                rows = pl.ds(r0 + q * SUBLANES_F32, CONV_ROWS)
                x = xs_ref[rows, cols] if r == 0 else sh_ref[r, rows, :]
                acc = acc + dw_ref[k:k + 1, cols] * x
            cv_ref[pl.ds(r0, CONV_ROWS), cols] = acc
    cv = cv_ref[...]
    mu = jnp.mean(cv, axis=-1, keepdims=True)
    xc = cv - mu
    var = jnp.mean(xc * xc, axis=-1, keepdims=True)
    y = xc * lax.rsqrt(var + EPS) * lg_ref[...] + lb_ref[...]
    y = y * _sigmoid(y)
    o_ref[0] = h_ref[0] + _dot(y.astype(bf16), wo_ref[...])


def odd_out(v3, dw, ln_g, ln_b, w_out, h3, ts):
    bsz, seq, inner = v3.shape
    d = h3.shape[2]
    taps = dw.shape[0]
    nh = ts // HALO
    last = seq // HALO - 1
    return pl.pallas_call(
        functools.partial(_odd_out_kernel, taps=taps),
        grid=(bsz, seq // ts),
        in_specs=[
            pl.BlockSpec((1, ts, inner), lambda b, i: (b, i, 0)),
            pl.BlockSpec((1, HALO, inner), lambda b, i: (b, jnp.maximum(i * nh - 1, 0), 0)),
            pl.BlockSpec((1, HALO, inner), lambda b, i: (b, jnp.minimum((i + 1) * nh, last), 0)),
            pl.BlockSpec(dw.shape, lambda b, i: (0, 0)),
            pl.BlockSpec((1, inner), lambda b, i: (0, 0)),
            pl.BlockSpec((1, inner), lambda b, i: (0, 0)),
            pl.BlockSpec(w_out.shape, lambda b, i: (0, 0)),
            pl.BlockSpec((1, ts, d), lambda b, i: (b, i, 0)),
        ],
        out_specs=pl.BlockSpec((1, ts, d), lambda b, i: (b, i, 0)),
        out_shape=jax.ShapeDtypeStruct((bsz, seq, d), f32),
        scratch_shapes=[
            pltpu.VMEM((ts + 2 * HALO, inner), f32),
            pltpu.VMEM((SUBLANES_F32, ts + SUBLANES_F32 * ((HALO + taps // 2) // SUBLANES_F32), CONV_COLS), f32),
            pltpu.VMEM((ts, inner), f32),
        ],
        compiler_params=_cp("parallel", "parallel"),
        name="odd_out",
    )(v3, v3, v3, dw, ln_g, ln_b, w_out, h3)


def _router_kernel(h_ref, g_ref, wh_ref, wl_ref, b_ref, x_ref, at_ref, *, n_exp):
    tm, d = h_ref.shape
    xn = _rms(h_ref[...], g_ref[...])
    xh = xn.astype(bf16)
    xl = (xn - xh.astype(f32)).astype(bf16)
    logits = _dot(xh, wh_ref[...]) + _dot(xl, wh_ref[...]) + _dot(xh, wl_ref[...]) + b_ref[...]
    lane = lax.broadcasted_iota(i32, logits.shape, 1)
    logits = jnp.where(lane < n_exp, logits, -1e30)
    m = jnp.max(logits, axis=-1, keepdims=True)
    p = jnp.exp(logits - m)
    aff = p / jnp.sum(p, axis=-1, keepdims=True)
    x_ref[:, 0:d] = xn
    x_ref[:, d:] = aff
    at = aff.T
    for q in range(tm // ROUTE_CHUNK):
        at_ref[q] = at[0:n_exp, q * ROUTE_CHUNK:(q + 1) * ROUTE_CHUNK]


def router(h, g, wh, wl, b, n_exp, tm):
    n, d = h.shape
    return pl.pallas_call(
        functools.partial(_router_kernel, n_exp=n_exp),
        grid=(n // tm,),
        in_specs=[
            pl.BlockSpec((tm, d), lambda i: (i, 0)),
            pl.BlockSpec((1, d), lambda i: (0, 0)),
            pl.BlockSpec((d, LANES), lambda i: (0, 0)),
            pl.BlockSpec((d, LANES), lambda i: (0, 0)),
            pl.BlockSpec((1, LANES), lambda i: (0, 0)),
        ],
        out_specs=[
            pl.BlockSpec((tm, d + LANES), lambda i: (i, 0)),
            pl.BlockSpec((tm // ROUTE_CHUNK, n_exp, ROUTE_CHUNK), lambda i: (i, 0, 0)),
        ],
        out_shape=[
            jax.ShapeDtypeStruct((n, d + LANES), f32),
            jax.ShapeDtypeStruct((n // ROUTE_CHUNK, n_exp, ROUTE_CHUNK), f32),
        ],
        compiler_params=_cp("parallel"),
        name="router",
    )(h, g, wh, wl, b)


def _select_kernel(a_ref, su_ref, sl_ref, pos_ref, rank_ref, aux_ref, cst_ref, bits_ref, *, cap):
    nchunk, n_exp, cw = a_ref.shape
    bits_ref[...] = pltpu.bitcast(a_ref[...], i32)

    def count_ge(cand):
        m = (bits_ref[...] >= cand[None]).astype(f32)
        return jnp.sum(jnp.sum(m, axis=0), axis=1, keepdims=True)

    def bisect(it, prefix):
        cand = prefix | jnp.left_shift(jnp.int32(1), 30 - it)
        return jnp.where(count_ge(cand) >= cap, cand, prefix)

    thr = lax.fori_loop(0, 31, bisect, jnp.zeros((n_exp, 1), i32))
    need = cap - (count_ge(thr + 1))

    sub = lax.broadcasted_iota(i32, (SUBLANES_F32, cw), 0)

    def chunk(c, carry):
        cg, cq = carry
        b = bits_ref[c]
        g = b > thr
        q = b == thr
        gq = jnp.concatenate([g.astype(f32), q.astype(f32)], axis=0).astype(bf16)
        pre = _dot(gq, su_ref[...])
        pg = cg + pre[:n_exp]
        pq = cq + pre[n_exp:]
        sel = g | (q & (pq < need))
        pos = pg + jnp.minimum(pq, need)
        self32 = sel.astype(f32)
        pos_ref[c] = jnp.where(sel, pos, -1.0).astype(i32)
        rtok = jnp.sum(pos, axis=0, keepdims=True)
        ktok = jnp.sum(self32, axis=0, keepdims=True)
        er = _dot(sl_ref[...], self32.astype(bf16))
        rank_ref[c] = (rtok + er).astype(i32)
        cstart = cg + jnp.minimum(cq, need)
        rstart = jnp.sum(cstart, axis=0, keepdims=True)
        aux = jnp.where(sub == 0, rtok, jnp.where(sub == 1, ktok, jnp.where(sub == 2, rstart, 0.0)))
        aux_ref[c] = aux.astype(i32)
        cst_ref[c] = jnp.broadcast_to(cstart, (n_exp, LANES)).astype(i32)
        return (cg + jnp.sum(g.astype(f32), axis=1, keepdims=True),
                cq + jnp.sum(q.astype(f32), axis=1, keepdims=True))

    zero = jnp.zeros((n_exp, 1), f32)
    lax.fori_loop(0, nchunk, chunk, (zero, zero))


def select(aff3, cap):
    nchunk, n_exp, cw = aff3.shape
    r = lax.broadcasted_iota(i32, (cw, cw), 0)
    c = lax.broadcasted_iota(i32, (cw, cw), 1)
    su = (r < c).astype(bf16)
    re = lax.broadcasted_iota(i32, (n_exp, n_exp), 0)
    ce = lax.broadcasted_iota(i32, (n_exp, n_exp), 1)
    sl = (ce < re).astype(bf16)
    full3 = lambda shape: pl.BlockSpec(shape, lambda i: (0, 0, 0))
    return pl.pallas_call(
        functools.partial(_select_kernel, cap=cap),
        grid=(1,),
        in_specs=[full3(aff3.shape), pl.BlockSpec(su.shape, lambda i: (0, 0)),
                  pl.BlockSpec(sl.shape, lambda i: (0, 0))],
        out_specs=[full3((nchunk, n_exp, cw)), full3((nchunk, n_exp, cw)),
                   full3((nchunk, SUBLANES_F32, cw)), full3((nchunk, n_exp, LANES))],
        out_shape=[
            jax.ShapeDtypeStruct((nchunk, n_exp, cw), i32),
            jax.ShapeDtypeStruct((nchunk, n_exp, cw), i32),
            jax.ShapeDtypeStruct((nchunk, SUBLANES_F32, cw), i32),
            jax.ShapeDtypeStruct((nchunk, n_exp, LANES), i32),
        ],
        scratch_shapes=[pltpu.VMEM((nchunk, n_exp, cw), i32)],
        compiler_params=_cp("arbitrary"),
        name="moe_select",
    )(aff3, su, sl)


def _compact_kernel(cst_ref, pos_ref, rank_ref, idx_ref, dst_ref, acc_ref):
    e = pl.program_id(0)
    nchunk, _, cw = pos_ref.shape
    nblk = idx_ref.shape[1]
    acc_ref[...] = jnp.zeros(acc_ref.shape, f32)
    sub = lax.broadcasted_iota(i32, (SUBLANES_BF16, cw), 0)
    slot = lax.broadcasted_iota(i32, (cw, cw), 0)
    lane = lax.broadcasted_iota(i32, (1, cw), 1)

    def chunk(c, _):
        first = cst_ref[e * (nchunk + 1) + c]
        end = cst_ref[e * (nchunk + 1) + c + 1]
        b0 = first // cw
        p = pos_ref[c, pl.ds(e, 1), :]
        rk = rank_ref[c, pl.ds(e, 1), :]
        tok = c * cw + lane
        vals = jnp.where(sub == 0, tok >> 8, jnp.where(sub == 1, tok & 255,
               jnp.where(sub == 2, rk >> 8, jnp.where(sub == 3, rk & 255, 0))))
        vt = vals.astype(f32).astype(bf16)

        def add_block(b):
            onehot = (p - b * cw == slot).astype(f32).astype(bf16)
            acc_ref[b] += _dot_nt(vt, onehot)

        @pl.when(end > first)
        def _():
            add_block(b0)

        @pl.when(end > (b0 + 1) * cw)
        def _():
            add_block(b0 + 1)

        return 0

    lax.fori_loop(0, nchunk, chunk, 0)
    for b in range(nblk):
        a = acc_ref[b]
        idx_ref[0, b:b + 1, :] = (a[0:1, :] * 256.0 + a[1:2, :]).astype(i32)
        dst_ref[0, b:b + 1, :] = (a[2:3, :] * 256.0 + a[3:4, :]).astype(i32)


def compact(cst_flat, pos3, rank3, cap):
    nchunk, n_exp, cw = pos3.shape
    nblk = cap // cw
    grid_spec = pltpu.PrefetchScalarGridSpec(
        num_scalar_prefetch=1,
        grid=(n_exp,),
        in_specs=[pl.BlockSpec(pos3.shape, lambda e, s: (0, 0, 0)),
                  pl.BlockSpec(rank3.shape, lambda e, s: (0, 0, 0))],
        out_specs=[pl.BlockSpec((1, nblk, cw), lambda e, s: (e, 0, 0)),
                   pl.BlockSpec((1, nblk, cw), lambda e, s: (e, 0, 0))],
        scratch_shapes=[pltpu.VMEM((nblk + 1, SUBLANES_BF16, cw), f32)],
    )
    return pl.pallas_call(
        _compact_kernel,
        grid_spec=grid_spec,
        out_shape=[jax.ShapeDtypeStruct((n_exp, nblk, cw), i32),
                   jax.ShapeDtypeStruct((n_exp, nblk, cw), i32)],
        compiler_params=_cp("arbitrary"),
        name="moe_compact",
    )(cst_flat, pos3, rank3)


def _expert_kernel(idx_ref, dst_ref, x_hbm, wg_ref, wu_ref, wd_ref, z_hbm,
                   xbuf, xb16, vcol, acc, ybuf, gsem, ssem, *, cap, nf_static):
    e = pl.program_id(0)
    m = pl.program_id(1)
    f = pl.program_id(2)
    nm = pl.num_programs(1)
    nf = pl.num_programs(2)
    bm, d = acc.shape
    base = e * cap + m * bm

    def gather_row(j):
        return pltpu.make_async_copy(x_hbm.at[pl.ds(idx_ref[base + j], 1), :], xbuf.at[pl.ds(j, 1), :], gsem)

    def scatter_row(j):
        return pltpu.make_async_copy(ybuf.at[pl.ds(j, 1), :], z_hbm.at[pl.ds(dst_ref[base + j], 1), :], ssem)

    def wait_gathers():
        pltpu.make_async_copy(x_hbm.at[pl.ds(0, bm), :], xbuf, gsem).wait()

    def wait_scatters():
        pltpu.make_async_copy(ybuf, z_hbm.at[pl.ds(0, bm), :], ssem).wait()

    @pl.when(f == 0)
    def _():
        def start(j, _):
            gather_row(j).start()
            return 0
        lax.fori_loop(0, bm, start, 0, unroll=ISSUE_UNROLL)
        wait_gathers()
        xb16[...] = xbuf[:, 0:d].astype(bf16)
        lane = lax.broadcasted_iota(i32, (bm, LANES), 1)
        vcol[...] = jnp.sum(jnp.where(lane == e, xbuf[:, d:], 0.0), axis=1, keepdims=True)

    x = xb16[...]
    gate = _dot(x, wg_ref[...])
    up = _dot(x, wu_ref[...])
    hid = (gate * _sigmoid(gate) * up).astype(bf16)
    part = _dot(hid, wd_ref[...])

    @pl.when(f == 0)
    def _():
        acc[...] = part

    @pl.when((f > 0) & (f < nf - 1))
    def _():
        acc[...] += part

    @pl.when(f == nf - 1)
    def _():
        @pl.when(e * nm + m > 0)
        def _():
            wait_scatters()

        ybuf[...] = (acc[...] + part) * vcol[...] if nf_static > 1 else part * vcol[...]

        def start(j, _):
            scatter_row(j).start()
            return 0
        lax.fori_loop(0, bm, start, 0, unroll=ISSUE_UNROLL)

        @pl.when((e == pl.num_programs(0) - 1) & (m == nm - 1))
        def _():
            wait_scatters()


def experts(idx, dst, xext, wg, wu, wd, layer, cap, bm, tf):
    _, n_exp, d, hidden = wg.shape
    grid_spec = pltpu.PrefetchScalarGridSpec(
        num_scalar_prefetch=2,
        grid=(n_exp, cap // bm, hidden // tf),
        in_specs=[
            pl.BlockSpec(memory_space=pl.ANY),
            pl.BlockSpec((None, None, d, tf), lambda e, m, f, i_, d_: (layer, e, 0, f)),
            pl.BlockSpec((None, None, d, tf), lambda e, m, f, i_, d_: (layer, e, 0, f)),
            pl.BlockSpec((None, None, tf, d), lambda e, m, f, i_, d_: (layer, e, f, 0)),
        ],
        out_specs=pl.BlockSpec(memory_space=pl.ANY),
        scratch_shapes=[
            pltpu.VMEM((bm, d + LANES), f32),
            pltpu.VMEM((bm, d), bf16),
            pltpu.VMEM((bm, 1), f32),
            pltpu.VMEM((bm, d), f32),
            pltpu.VMEM((bm, d), f32),
            pltpu.SemaphoreType.DMA,
            pltpu.SemaphoreType.DMA,
        ],
    )
    return pl.pallas_call(
        functools.partial(_expert_kernel, cap=cap, nf_static=hidden // tf),
        grid_spec=grid_spec,
        out_shape=jax.ShapeDtypeStruct((n_exp * cap, d), f32),
        compiler_params=_cp("arbitrary", "arbitrary", "arbitrary"),
        name="moe_experts",
    )(idx, dst, xext, wg, wu, wd)


def _combine_kernel(ts_ref, h_ref, aux_ref, eye_ref, z_hbm, o_ref, zbuf, sem, slot_ref, *, total):
    i = pl.program_id(0)
    nt = pl.num_programs(0)
    tm, d = h_ref.shape
    cw = zbuf.shape[1]
    sub = lax.broadcasted_iota(i32, (SUBLANES_BF16, cw), 0)
    rt = aux_ref[0, 0:1, :]
    kt = aux_ref[0, 1:2, :]
    parts = jnp.where(sub == 0, rt >> 8, jnp.where(sub == 1, rt & 255, jnp.where(sub == 2, kt, 0)))
    col = _dot_nt(eye_ref[...], parts.astype(f32).astype(bf16))
    rlo = col[:, 0:1] * 256.0 + col[:, 1:2]
    rhi = rlo + col[:, 2:3]
    def first_row(t):
        return (ts_ref[t] // SUBLANES_F32) * SUBLANES_F32

    def window(nominal):
        return pl.multiple_of(jnp.minimum(nominal, total - cw), SUBLANES_F32)

    def chunk_copy(nominal, slot):
        return pltpu.make_async_copy(z_hbm.at[pl.ds(window(nominal), cw), :], zbuf.at[slot], sem.at[slot])

    a0 = first_row(i)
    nch = jnp.maximum((ts_ref[i + 1] - a0 + cw - 1) // cw, 1)

    @pl.when(i == 0)
    def _():
        slot_ref[0] = 0
        chunk_copy(a0, 0).start()

    s0 = slot_ref[0]
    o_ref[...] = h_ref[...]
    lane = lax.broadcasted_iota(i32, (1, cw), 1)

    def chunk(k, _):
        slot = (s0 + k) % 2
        nominal = a0 + k * cw

        @pl.when(k + 1 < nch)
        def _():
            chunk_copy(nominal + cw, 1 - slot).start()

        @pl.when((k + 1 == nch) & (i + 1 < nt))
        def _():
            chunk_copy(first_row(i + 1), 1 - slot).start()
            slot_ref[0] = 1 - slot

        chunk_copy(nominal, slot).wait()
        row = window(nominal) + lane
        rowf = row.astype(f32)
        onehot = ((rowf >= rlo) & (rowf < rhi) & (row >= nominal)).astype(f32).astype(bf16)
        o_ref[...] += _dot(onehot, zbuf[slot].astype(bf16))
        return 0

    lax.fori_loop(0, nch, chunk, 0)


def combine(tstart, h, aux3, z):
    n, d = h.shape
    cw = aux3.shape[2]
    total = z.shape[0]
    eye = jnp.eye(cw, dtype=bf16)
    grid_spec = pltpu.PrefetchScalarGridSpec(
        num_scalar_prefetch=1,
        grid=(n // cw,),
        in_specs=[
            pl.BlockSpec((cw, d), lambda i, s: (i, 0)),
            pl.BlockSpec((1,) + aux3.shape[1:], lambda i, s: (i, 0, 0)),
            pl.BlockSpec(eye.shape, lambda i, s: (0, 0)),
            pl.BlockSpec(memory_space=pl.ANY),
        ],
        out_specs=pl.BlockSpec((cw, d), lambda i, s: (i, 0)),
        scratch_shapes=[pltpu.VMEM((2, cw, d), f32), pltpu.SemaphoreType.DMA((2,)), pltpu.SMEM((1,), i32)],
    )
    return pl.pallas_call(
        functools.partial(_combine_kernel, total=total),
        grid_spec=grid_spec,
        out_shape=jax.ShapeDtypeStruct((n, d), f32),
        compiler_params=_cp("arbitrary"),
        name="moe_combine",
    )(tstart, h, aux3, eye, z)


def moe(h, g, wr_hi, wr_lo, b_r, wg, wu, wd, layer, n_exp, tm, bm, tf):
    n, d = h.shape
    cap = CAPACITY_FACTOR * n // n_exp
    xext, aff3 = router(h, g, wr_hi, wr_lo, b_r, n_exp, tm)
    pos3, rank3, aux3, cst3 = select(aff3, cap)
    cst_flat = jnp.concatenate([cst3[:, :, 0].T, jnp.full((n_exp, 1), cap, i32)], axis=1).reshape(-1)
    idx, dst = compact(cst_flat, pos3, rank3, cap)
    z = experts(idx.reshape(-1), dst.reshape(-1), xext, wg, wu, wd, layer, cap, bm, tf)
    tstart = jnp.concatenate([aux3[:, 2, 0], jnp.full((1,), n_exp * cap, i32)])
    return combine(tstart, h, aux3, z)


def _ple_kernel(h_ref, p_ref, g_ref, wg_ref, wp_ref, gf_ref, o_ref, *, final):
    h = h_ref[...]
    gate = _sigmoid(_dot(_rms(h, g_ref[...]).astype(bf16), wg_ref[...]))
    out = h + gate * _dot(p_ref[...].astype(bf16), wp_ref[...])
    if final:
        out = _rms(out, gf_ref[...])
    o_ref[...] = out


def ple(h, p, g, w_gate, w_proj, g_final, final, tm):
    n, d = h.shape
    pd = p.shape[1]
    return pl.pallas_call(
        functools.partial(_ple_kernel, final=final),
        grid=(n // tm,),
        in_specs=[
            pl.BlockSpec((tm, d), lambda i: (i, 0)),
            pl.BlockSpec((tm, pd), lambda i: (i, 0)),
            pl.BlockSpec((1, d), lambda i: (0, 0)),
            pl.BlockSpec((d, d), lambda i: (0, 0)),
            pl.BlockSpec((pd, d), lambda i: (0, 0)),
            pl.BlockSpec((1, d), lambda i: (0, 0)),
        ],
        out_specs=pl.BlockSpec((tm, d), lambda i: (i, 0)),
        out_shape=jax.ShapeDtypeStruct((n, d), f32),
        compiler_params=_cp("parallel"),
        name="ple_final" if final else "ple",
    )(h, p, g, w_gate, w_proj, g_final)


def _tiles(n, seq, cap):
    tm = min(512, n)
    ts = min(256, seq)
    bm = min(512, cap)
    return tm, ts, bm


def _run_trunk(x, p, wts, n_exp):
    bsz, seq, d = x.shape
    depth = p.shape[0]
    n = bsz * seq
    cap = CAPACITY_FACTOR * n // n_exp
    tm, ts, bm = _tiles(n, seq, cap)
    hidden = wts["w_gate"].shape[-1]
    tf = min(1024, hidden)
    inner = wts["w_c_out"].shape[1]
    tn = min(1024, inner)
    aw = wts["w_a_mix"].shape[1] * wts["w_a_mix"].shape[2]
    ch = wts["w_a_mix"].shape[2]
    s2 = DFT_INNER
    s1 = seq // s2
    tb = min(s2, max(SUBLANES_F32, DFT_ROWS // s1))
    k1, t2, cs = _dft_tables(seq, tb, ch)
    cb = min(512, aw)

    h = x.reshape(n, d)
    for i in range(depth):
        j = i // 2
        if i % 2 == 0:
            ua, ub = even_in(h, wts["g_mix"][i], wts["w_ab_in"][j], tm)
            a5 = dft_stage1(ua.reshape(bsz, s1, s2, aw), k1, tb, cb)
            yp = dft_stage2(a5, t2, cs, wts["w_a_mix"][j])
            ya = jnp.swapaxes(yp, 1, 2).reshape(bsz, seq, aw)
            h = even_out(ub.reshape(bsz, seq, -1), ya, h.reshape(bsz, seq, d), wts["w_b_mix"][j],
                         wts["b_scale"][j], wts["w_ab_out"][j], ts).reshape(n, d)
        else:
            v = odd_in(h, wts["g_mix"][i], wts["w_c_in"][j], tm, tn)
            h = odd_out(v.reshape(bsz, seq, inner), wts["c_dw"][j], wts["c_ln_g"][j], wts["c_ln_b"][j],
                        wts["w_c_out"][j], h.reshape(bsz, seq, d), ts).reshape(n, d)
        h = moe(h, wts["g_ffn"][i], wts["w_router_hi"][i], wts["w_router_lo"][i], wts["b_router"][i],
                wts["w_gate"], wts["w_up"], wts["w_down"], i, n_exp, tm, bm, tf)
        h = ple(h, p[i].reshape(n, -1), wts["g_ple"][i], wts["w_ple_gate"][i], wts["w_ple_proj"][i],
                wts["g_final"], i == depth - 1, tm)
    return h.reshape(bsz, seq, d)


def kernel(x_prompt, x_sample, p_prompt, p_sample, g_mix, w_ab_in, w_a_mix, w_b_mix, b_scale, w_ab_out,
           w_c_in, c_dw, c_ln_g, c_ln_b, w_c_out, g_ffn, w_router, b_router, w_gate, w_up, w_down,
           g_ple, w_ple_gate, w_ple_proj, g_final):
    depth, d = g_mix.shape
    n_exp = w_router.shape[-1]
    wr = jnp.pad(w_router, ((0, 0), (0, 0), (0, LANES - n_exp)))
    wr_hi = wr.astype(bf16)
    wr_lo = (wr - wr_hi.astype(f32)).astype(bf16)
    wts = dict(
        g_mix=g_mix[:, None, :],
        w_ab_in=w_ab_in.astype(bf16),
        w_a_mix=w_a_mix.astype(bf16),
        w_b_mix=w_b_mix.astype(bf16),
        b_scale=b_scale[:, None, :],
        w_ab_out=w_ab_out.astype(bf16),
        w_c_in=w_c_in.astype(bf16),
        c_dw=c_dw,
        c_ln_g=c_ln_g[:, None, :],
        c_ln_b=c_ln_b[:, None, :],
        w_c_out=w_c_out.astype(bf16),
        g_ffn=g_ffn[:, None, :],
        w_router_hi=wr_hi,
        w_router_lo=wr_lo,
        b_router=jnp.pad(b_router, ((0, 0), (0, LANES - n_exp)))[:, None, :],
        w_gate=w_gate.astype(bf16),
        w_up=w_up.astype(bf16),
        w_down=w_down.astype(bf16),
        g_ple=g_ple[:, None, :],
        w_ple_gate=w_ple_gate.astype(bf16),
        w_ple_proj=w_ple_proj.astype(bf16),
        g_final=g_final[None, :],
    )
    y_prompt = _run_trunk(x_prompt, p_prompt, wts, n_exp)
    y_sample = _run_trunk(x_sample, p_sample, wts, n_exp)
    return (y_prompt, y_sample)
```

```python
import functools

import numpy as np
import jax
import jax.numpy as jnp
from jax import lax
from jax.experimental import pallas as pl
from jax.experimental.pallas import tpu as pltpu

f32 = jnp.float32
bf16 = jnp.bfloat16
i32 = jnp.int32

EPS = 1e-6
CAPACITY_FACTOR = 2
N_FOURIER_GROUPS = 4
POOL_WINDOWS = (2, 4, 8, 16)

LANES = 128
SUBLANES_F32 = 8
SUBLANES_BF16 = 16
MXU_DIM = 256
VMEM_LIMIT_BYTES = 56 * 1024 * 1024
EXPERT_VMEM_LIMIT_BYTES = 60 * 1024 * 1024

ROUTE_CHUNK = MXU_DIM
DFT_INNER = 128
DFT_ROWS = 1024
HALO = 16
ISSUE_UNROLL = 8


def _cp(*dims):
    return pltpu.CompilerParams(dimension_semantics=dims, vmem_limit_bytes=VMEM_LIMIT_BYTES)


def _rms(x, g):
    ms = jnp.mean(x * x, axis=-1, keepdims=True)
    return x * lax.rsqrt(ms + EPS) * g


def _sigmoid(x):
    return 1.0 / (1.0 + jnp.exp(-x))


def _dot(a, b):
    return jnp.dot(a, b, preferred_element_type=f32)


def _dot_nt(a, b):
    return lax.dot_general(a, b, (((1,), (1,)), ((), ())), preferred_element_type=f32)


def _even_in_kernel(h_ref, g_ref, w_ref, ua_ref, ub_ref):
    y = _rms(h_ref[...], g_ref[...]).astype(bf16)
    u = _dot(y, w_ref[...])
    half = ua_ref.shape[-1]
    ua_ref[...] = u[:, :half]
    ub_ref[...] = u[:, half:].astype(bf16)


def even_in(h, g, w, tm):
    n, d = h.shape
    mix = w.shape[1]
    half = mix // 2
    return pl.pallas_call(
        _even_in_kernel,
        grid=(n // tm,),
        in_specs=[
            pl.BlockSpec((tm, d), lambda i: (i, 0)),
            pl.BlockSpec((1, d), lambda i: (0, 0)),
            pl.BlockSpec((d, mix), lambda i: (0, 0)),
        ],
        out_specs=[
            pl.BlockSpec((tm, half), lambda i: (i, 0)),
            pl.BlockSpec((tm, half), lambda i: (i, 0)),
        ],
        out_shape=[jax.ShapeDtypeStruct((n, half), f32), jax.ShapeDtypeStruct((n, half), bf16)],
        compiler_params=_cp("parallel"),
        name="even_in",
    )(h, g, w)


def _dft_tables(seq, tb, ch):
    s2 = DFT_INNER
    s1 = seq // s2
    a = jnp.arange(s1, dtype=i32)
    ang1 = (2.0 * np.pi / s1) * ((a[:, None] * a[None, :]) % s1).astype(f32)
    f1 = jnp.concatenate([jnp.cos(ang1), -jnp.sin(ang1)], axis=0) * (1.0 / np.sqrt(s1))
    k1 = jnp.kron(f1, jnp.eye(tb, dtype=f32)).astype(bf16)
    b = jnp.arange(s2, dtype=i32)
    k = (b[None, None, :] * (b[None, :, None] * s1 + a[:, None, None])) % seq
    phi = (2.0 * np.pi / seq) * k.astype(f32)
    mr = jnp.cos(phi) * (1.0 / np.sqrt(s2))
    mi = -jnp.sin(phi) * (1.0 / np.sqrt(s2))
    t2 = jnp.concatenate(
        [jnp.concatenate([mr, -mi], axis=2), jnp.concatenate([mi, mr], axis=2)], axis=1
    ).astype(bf16)
    c = jnp.arange(ch, dtype=i32)
    th = (2.0 * np.pi / ch) * ((c[:, None] * c[None, :]) % ch).astype(f32)
    cs = (jnp.concatenate([jnp.cos(th), jnp.sin(th)], axis=0) * (1.0 / np.sqrt(ch))).astype(bf16)
    return k1, t2, cs


def _dft1_kernel(x_ref, k_ref, o_ref):
    _, s1, tb, cb = x_ref.shape
    x = x_ref[0].reshape(s1 * tb, cb).astype(bf16)
    r = _dot(k_ref[...], x)
    o_ref[0] = r.reshape(2, s1, tb, cb)


def dft_stage1(ua4, k1, tb, cb):
    bsz, s1, s2, c = ua4.shape
    return pl.pallas_call(
        _dft1_kernel,
        grid=(bsz, s2 // tb, c // cb),
        in_specs=[
            pl.BlockSpec((1, s1, tb, cb), lambda b, j, k: (b, 0, j, k)),
            pl.BlockSpec(k1.shape, lambda b, j, k: (0, 0)),
        ],
        out_specs=pl.BlockSpec((1, 2, s1, tb, cb), lambda b, j, k: (b, 0, 0, j, k)),
        out_shape=jax.ShapeDtypeStruct((bsz, 2, s1, s2, c), f32),
        compiler_params=_cp("parallel", "parallel", "parallel"),
        name="dft_stage1",
    )(ua4, k1)


def _dft2_kernel(a_ref, t_ref, cs_ref, w_ref, o_ref):
    s2, c = a_ref.shape[3], a_ref.shape[4]
    groups, ch, _ = w_ref.shape
    a = a_ref[0, :, 0].reshape(2 * s2, c).astype(bf16)
    p = _dot(t_ref[0], a)
    for g in range(groups):
        cols = slice(g * ch, (g + 1) * ch)
        pre = p[:s2, cols].astype(bf16)
        pim = p[s2:, cols].astype(bf16)
        f = _dot(pre, cs_ref[:ch, :]) + _dot(pim, cs_ref[ch:, :])
        y = _dot(f.astype(bf16), w_ref[g])
        o_ref[0, 0, :, cols] = y.astype(o_ref.dtype)


def dft_stage2(a5, t2, cs, w_a):
    bsz, _, s1, s2, c = a5.shape
    return pl.pallas_call(
        _dft2_kernel,
        grid=(bsz, s1),
        in_specs=[
            pl.BlockSpec((1, 2, 1, s2, c), lambda b, j: (b, 0, j, 0, 0)),
            pl.BlockSpec((1,) + t2.shape[1:], lambda b, j: (j, 0, 0)),
            pl.BlockSpec(cs.shape, lambda b, j: (0, 0)),
            pl.BlockSpec(w_a.shape, lambda b, j: (0, 0, 0)),
        ],
        out_specs=pl.BlockSpec((1, 1, s2, c), lambda b, j: (b, j, 0, 0)),
        out_shape=jax.ShapeDtypeStruct((bsz, s1, s2, c), bf16),
        compiler_params=_cp("parallel", "parallel"),
        name="dft_stage2",
    )(a5, t2, cs, w_a)


def _even_out_kernel(ub_ref, ubp_ref, ubn_ref, ya_ref, h_ref, wb_ref, bs_ref, wo_ref, o_ref, xs_ref,
                     *, seq):
    i = pl.program_id(1)
    nt = pl.num_programs(1)
    ts = ub_ref.shape[1]
    groups, ch, _ = wb_ref.shape
    aw = ya_ref.shape[2]
    xs_ref[0:HALO, :] = jnp.where(i > 0, ubp_ref[0].astype(f32), 0.0)
    xs_ref[HALO:HALO + ts, :] = ub_ref[0].astype(f32)
    xs_ref[HALO + ts:, :] = jnp.where(i < nt - 1, ubn_ref[0].astype(f32), 0.0)
    t = i * ts + lax.broadcasted_iota(i32, (ts, 1), 0)
    acc = h_ref[0] + _dot(ya_ref[0], wo_ref[0:aw, :])
    for g in range(groups):
        w = POOL_WINDOWS[g]
        cols = pl.ds(g * ch, ch)
        s = xs_ref[pl.ds(HALO - w // 2, ts), cols]
        for o in range(-w // 2 + 1, w // 2):
            s = s + xs_ref[pl.ds(HALO + o, ts), cols]
        cnt = (jnp.minimum(t + w // 2, seq) - jnp.maximum(t - w // 2, 0)).astype(f32)
        pooled = s / cnt - xs_ref[pl.ds(HALO, ts), cols]
        yb = _dot(pooled.astype(bf16), wb_ref[g]) * bs_ref[:, cols]
        acc = acc + _dot(yb.astype(bf16), wo_ref[pl.ds(aw + g * ch, ch), :])
    o_ref[0] = acc


def even_out(ub3, ya3, h3, w_b, b_scale, w_out, ts):
    bsz, seq, bw = ub3.shape
    d = h3.shape[2]
    aw = ya3.shape[2]
    nh = ts // HALO
    last = seq // HALO - 1
    return pl.pallas_call(
        functools.partial(_even_out_kernel, seq=seq),
        grid=(bsz, seq // ts),
        in_specs=[
            pl.BlockSpec((1, ts, bw), lambda b, i: (b, i, 0)),
            pl.BlockSpec((1, HALO, bw), lambda b, i: (b, jnp.maximum(i * nh - 1, 0), 0)),
            pl.BlockSpec((1, HALO, bw), lambda b, i: (b, jnp.minimum((i + 1) * nh, last), 0)),
            pl.BlockSpec((1, ts, aw), lambda b, i: (b, i, 0)),
            pl.BlockSpec((1, ts, d), lambda b, i: (b, i, 0)),
            pl.BlockSpec(w_b.shape, lambda b, i: (0, 0, 0)),
            pl.BlockSpec((1, bw), lambda b, i: (0, 0)),
            pl.BlockSpec(w_out.shape, lambda b, i: (0, 0)),
        ],
        out_specs=pl.BlockSpec((1, ts, d), lambda b, i: (b, i, 0)),
        out_shape=jax.ShapeDtypeStruct((bsz, seq, d), f32),
        scratch_shapes=[pltpu.VMEM((ts + 2 * HALO, bw), f32)],
        compiler_params=_cp("parallel", "parallel"),
        name="even_out",
    )(ub3, ub3, ub3, ya3, h3, w_b, b_scale, w_out)


def _odd_in_kernel(h_ref, g_ref, wa_ref, wb_ref, v_ref):
    y = _rms(h_ref[...], g_ref[...]).astype(bf16)
    a = _dot(y, wa_ref[...])
    b = _dot(y, wb_ref[...])
    v_ref[...] = a * _sigmoid(b)


def odd_in(h, g, w, tm, tn):
    n, d = h.shape
    inner = w.shape[1] // 2
    ncol = inner // tn
    return pl.pallas_call(
        _odd_in_kernel,
        grid=(ncol, n // tm),
        in_specs=[
            pl.BlockSpec((tm, d), lambda j, i: (i, 0)),
            pl.BlockSpec((1, d), lambda j, i: (0, 0)),
            pl.BlockSpec((d, tn), lambda j, i: (0, j)),
            pl.BlockSpec((d, tn), lambda j, i: (0, j + ncol)),
        ],
        out_specs=pl.BlockSpec((tm, tn), lambda j, i: (i, j)),
        out_shape=jax.ShapeDtypeStruct((n, inner), f32),
        compiler_params=_cp("parallel", "parallel"),
        name="odd_in",
    )(h, g, w, w)


CONV_ROWS = 32
CONV_COLS = 512


def _odd_out_kernel(v_ref, vp_ref, vn_ref, dw_ref, lg_ref, lb_ref, wo_ref, h_ref, o_ref, xs_ref, sh_ref,
                    cv_ref, *, taps):
    i = pl.program_id(1)
    nt = pl.num_programs(1)
    ts, inner = v_ref.shape[1], v_ref.shape[2]
    first = HALO - taps // 2
    span = sh_ref.shape[1]
    xs_ref[0:HALO, :] = jnp.where(i > 0, vp_ref[0], 0.0)
    xs_ref[HALO:HALO + ts, :] = v_ref[0]
    xs_ref[HALO + ts:, :] = jnp.where(i < nt - 1, vn_ref[0], 0.0)
    for c0 in range(0, inner, CONV_COLS):
        cols = pl.ds(c0, CONV_COLS)
        for r in range(1, SUBLANES_F32):
            sh_ref[r] = xs_ref[pl.ds(r, span), cols]
        for r0 in range(0, ts, CONV_ROWS):
            acc = jnp.zeros((CONV_ROWS, CONV_COLS), f32)
            for k in range(taps):
                q, r = divmod(first + k, SUBLANES_F32)
                rows = pl.ds(r0 + q * SUBLANES_F32, CONV_ROWS)
                x = xs_ref[rows, cols] if r == 0 else sh_ref[r, rows, :]
                acc = acc + dw_ref[k:k + 1, cols] * x
            cv_ref[pl.ds(r0, CONV_ROWS), cols] = acc
    cv = cv_ref[...]
    mu = jnp.mean(cv, axis=-1, keepdims=True)
    xc = cv - mu
    var = jnp.mean(xc * xc, axis=-1, keepdims=True)
    y = xc * lax.rsqrt(var + EPS) * lg_ref[...] + lb_ref[...]
    y = y * _sigmoid(y)
    o_ref[0] = h_ref[0] + _dot(y.astype(bf16), wo_ref[...])


def odd_out(v3, dw, ln_g, ln_b, w_out, h3, ts):
    bsz, seq, inner = v3.shape
    d = h3.shape[2]
    taps = dw.shape[0]
    nh = ts // HALO
    last = seq // HALO - 1
    return pl.pallas_call(
        functools.partial(_odd_out_kernel, taps=taps),
        grid=(bsz, seq // ts),
        in_specs=[
            pl.BlockSpec((1, ts, inner), lambda b, i: (b, i, 0)),
            pl.BlockSpec((1, HALO, inner), lambda b, i: (b, jnp.maximum(i * nh - 1, 0), 0)),
            pl.BlockSpec((1, HALO, inner), lambda b, i: (b, jnp.minimum((i + 1) * nh, last), 0)),
            pl.BlockSpec(dw.shape, lambda b, i: (0, 0)),
            pl.BlockSpec((1, inner), lambda b, i: (0, 0)),
            pl.BlockSpec((1, inner), lambda b, i: (0, 0)),
            pl.BlockSpec(w_out.shape, lambda b, i: (0, 0)),
            pl.BlockSpec((1, ts, d), lambda b, i: (b, i, 0)),
        ],
        out_specs=pl.BlockSpec((1, ts, d), lambda b, i: (b, i, 0)),
        out_shape=jax.ShapeDtypeStruct((bsz, seq, d), f32),
        scratch_shapes=[
            pltpu.VMEM((ts + 2 * HALO, inner), f32),
            pltpu.VMEM((SUBLANES_F32, ts + SUBLANES_F32 * ((HALO + taps // 2) // SUBLANES_F32), CONV_COLS), f32),
            pltpu.VMEM((ts, inner), f32),
        ],
        compiler_params=_cp("parallel", "parallel"),
        name="odd_out",
    )(v3, v3, v3, dw, ln_g, ln_b, w_out, h3)


def _router_kernel(h_ref, g_ref, wh_ref, wl_ref, b_ref, x_ref, at_ref, *, n_exp):
    tm, d = h_ref.shape
    xn = _rms(h_ref[...], g_ref[...])
    xh = xn.astype(bf16)
    xl = (xn - xh.astype(f32)).astype(bf16)
    logits = _dot(xh, wh_ref[...]) + _dot(xl, wh_ref[...]) + _dot(xh, wl_ref[...]) + b_ref[...]
    lane = lax.broadcasted_iota(i32, logits.shape, 1)
    logits = jnp.where(lane < n_exp, logits, -1e30)
    m = jnp.max(logits, axis=-1, keepdims=True)
    p = jnp.exp(logits - m)
    aff = p / jnp.sum(p, axis=-1, keepdims=True)
    x_ref[:, 0:d] = xn
    x_ref[:, d:] = aff
    at = aff.T
    for q in range(tm // ROUTE_CHUNK):
        at_ref[q] = at[0:n_exp, q * ROUTE_CHUNK:(q + 1) * ROUTE_CHUNK]


def router(h, g, wh, wl, b, n_exp, tm):
    n, d = h.shape
    return pl.pallas_call(
        functools.partial(_router_kernel, n_exp=n_exp),
        grid=(n // tm,),
        in_specs=[
            pl.BlockSpec((tm, d), lambda i: (i, 0)),
            pl.BlockSpec((1, d), lambda i: (0, 0)),
            pl.BlockSpec((d, LANES), lambda i: (0, 0)),
            pl.BlockSpec((d, LANES), lambda i: (0, 0)),
            pl.BlockSpec((1, LANES), lambda i: (0, 0)),
        ],
        out_specs=[
            pl.BlockSpec((tm, d + LANES), lambda i: (i, 0)),
            pl.BlockSpec((tm // ROUTE_CHUNK, n_exp, ROUTE_CHUNK), lambda i: (i, 0, 0)),
        ],
        out_shape=[
            jax.ShapeDtypeStruct((n, d + LANES), f32),
            jax.ShapeDtypeStruct((n // ROUTE_CHUNK, n_exp, ROUTE_CHUNK), f32),
        ],
        compiler_params=_cp("parallel"),
        name="router",
    )(h, g, wh, wl, b)


def _select_kernel(a_ref, su_ref, sl_ref, pos_ref, rank_ref, aux_ref, cst_ref, bits_ref, *, cap):
    nchunk, n_exp, cw = a_ref.shape
    bits_ref[...] = pltpu.bitcast(a_ref[...], i32)

    def count_ge(cand):
        m = (bits_ref[...] >= cand[None]).astype(f32)
        return jnp.sum(jnp.sum(m, axis=0), axis=1, keepdims=True)

    def bisect(it, prefix):
        cand = prefix | jnp.left_shift(jnp.int32(1), 30 - it)
        return jnp.where(count_ge(cand) >= cap, cand, prefix)

    thr = lax.fori_loop(0, 31, bisect, jnp.zeros((n_exp, 1), i32))
    need = cap - (count_ge(thr + 1))

    sub = lax.broadcasted_iota(i32, (SUBLANES_F32, cw), 0)

    def chunk(c, carry):
        cg, cq = carry
        b = bits_ref[c]
        g = b > thr
        q = b == thr
        gq = jnp.concatenate([g.astype(f32), q.astype(f32)], axis=0).astype(bf16)
        pre = _dot(gq, su_ref[...])
        pg = cg + pre[:n_exp]
        pq = cq + pre[n_exp:]
        sel = g | (q & (pq < need))
        pos = pg + jnp.minimum(pq, need)
        self32 = sel.astype(f32)
        pos_ref[c] = jnp.where(sel, pos, -1.0).astype(i32)
        rtok = jnp.sum(pos, axis=0, keepdims=True)
        ktok = jnp.sum(self32, axis=0, keepdims=True)
        er = _dot(sl_ref[...], self32.astype(bf16))
        rank_ref[c] = (rtok + er).astype(i32)
        cstart = cg + jnp.minimum(cq, need)
        rstart = jnp.sum(cstart, axis=0, keepdims=True)
        aux = jnp.where(sub == 0, rtok, jnp.where(sub == 1, ktok, jnp.where(sub == 2, rstart, 0.0)))
        aux_ref[c] = aux.astype(i32)
        cst_ref[c] = jnp.broadcast_to(cstart, (n_exp, LANES)).astype(i32)
        return (cg + jnp.sum(g.astype(f32), axis=1, keepdims=True),
                cq + jnp.sum(q.astype(f32), axis=1, keepdims=True))

    zero = jnp.zeros((n_exp, 1), f32)
    lax.fori_loop(0, nchunk, chunk, (zero, zero))


def select(aff3, cap):
    nchunk, n_exp, cw = aff3.shape
    r = lax.broadcasted_iota(i32, (cw, cw), 0)
    c = lax.broadcasted_iota(i32, (cw, cw), 1)
    su = (r < c).astype(bf16)
    re = lax.broadcasted_iota(i32, (n_exp, n_exp), 0)
    ce = lax.broadcasted_iota(i32, (n_exp, n_exp), 1)
    sl = (ce < re).astype(bf16)
    full3 = lambda shape: pl.BlockSpec(shape, lambda i: (0, 0, 0))
    return pl.pallas_call(
        functools.partial(_select_kernel, cap=cap),
        grid=(1,),
        in_specs=[full3(aff3.shape), pl.BlockSpec(su.shape, lambda i: (0, 0)),
                  pl.BlockSpec(sl.shape, lambda i: (0, 0))],
        out_specs=[full3((nchunk, n_exp, cw)), full3((nchunk, n_exp, cw)),
                   full3((nchunk, SUBLANES_F32, cw)), full3((nchunk, n_exp, LANES))],
        out_shape=[
            jax.ShapeDtypeStruct((nchunk, n_exp, cw), i32),
            jax.ShapeDtypeStruct((nchunk, n_exp, cw), i32),
            jax.ShapeDtypeStruct((nchunk, SUBLANES_F32, cw), i32),
            jax.ShapeDtypeStruct((nchunk, n_exp, LANES), i32),
        ],
        scratch_shapes=[pltpu.VMEM((nchunk, n_exp, cw), i32)],
        compiler_params=_cp("arbitrary"),
        name="moe_select",
    )(aff3, su, sl)


def _compact_kernel(cst_ref, pos_ref, rank_ref, idx_ref, dst_ref, acc_ref):
    e = pl.program_id(0)
    nchunk, _, cw = pos_ref.shape
    nblk = idx_ref.shape[1]
    acc_ref[...] = jnp.zeros(acc_ref.shape, f32)
    sub = lax.broadcasted_iota(i32, (SUBLANES_BF16, cw), 0)
    slot = lax.broadcasted_iota(i32, (cw, cw), 0)
    lane = lax.broadcasted_iota(i32, (1, cw), 1)

    def chunk(c, _):
        first = cst_ref[e * (nchunk + 1) + c]
        b0 = first // cw
        p = pos_ref[c, pl.ds(e, 1), :]
        rk = rank_ref[c, pl.ds(e, 1), :]
        tok = c * cw + lane
        vals = jnp.where(sub == 0, tok >> 8, jnp.where(sub == 1, tok & 255,
               jnp.where(sub == 2, rk >> 8, jnp.where(sub == 3, rk & 255, 0))))
        vt = vals.astype(f32).astype(bf16)

        for b in (b0, b0 + 1):
            onehot = (p - b * cw == slot).astype(f32).astype(bf16)
            acc_ref[b] += _dot_nt(vt, onehot)
        return 0

    lax.fori_loop(0, nchunk, chunk, 0, unroll=2)
    for b in range(nblk):
        a = acc_ref[b]
        idx_ref[0, b:b + 1, :] = (a[0:1, :] * 256.0 + a[1:2, :]).astype(i32)
        dst_ref[0, b:b + 1, :] = (a[2:3, :] * 256.0 + a[3:4, :]).astype(i32)


def compact(cst_flat, pos3, rank3, cap):
    nchunk, n_exp, cw = pos3.shape
    nblk = cap // cw
    grid_spec = pltpu.PrefetchScalarGridSpec(
        num_scalar_prefetch=1,
        grid=(n_exp,),
        in_specs=[pl.BlockSpec(pos3.shape, lambda e, s: (0, 0, 0)),
                  pl.BlockSpec(rank3.shape, lambda e, s: (0, 0, 0))],
        out_specs=[pl.BlockSpec((1, nblk, cw), lambda e, s: (e, 0, 0)),
                   pl.BlockSpec((1, nblk, cw), lambda e, s: (e, 0, 0))],
        scratch_shapes=[pltpu.VMEM((nblk + 1, SUBLANES_BF16, cw), f32)],
    )
    return pl.pallas_call(
        _compact_kernel,
        grid_spec=grid_spec,
        out_shape=[jax.ShapeDtypeStruct((n_exp, nblk, cw), i32),
                   jax.ShapeDtypeStruct((n_exp, nblk, cw), i32)],
        compiler_params=_cp("arbitrary"),
        name="moe_compact",
    )(cst_flat, pos3, rank3)


def _expert_kernel(idx_ref, dst_ref, x_hbm, wg_ref, wu_ref, wd_ref, z_hbm,
                   xbuf, xb16, vcol, ybuf, gsem, ssem, *, nf_static):
    e = pl.program_id(0)
    m = pl.program_id(1)
    f = pl.program_id(2)
    nm = pl.num_programs(1)
    nf = pl.num_programs(2)
    _, bm, d = ybuf.shape
    blk = e * nm + m
    last_blk = pl.num_programs(0) * nm - 1
    s = blk % 2
    rows = bm // nf_static

    def gather_row(b, j, slot):
        return pltpu.make_async_copy(x_hbm.at[pl.ds(idx_ref[b * bm + j], 1), :],
                                     xbuf.at[slot, pl.ds(j, 1), :], gsem.at[slot])

    def scatter_row(b, j, slot):
        return pltpu.make_async_copy(ybuf.at[slot, pl.ds(j, 1), :],
                                     z_hbm.at[pl.ds(dst_ref[b * bm + j], 1), :], ssem.at[slot])

    def wait_gathers(slot):
        pltpu.make_async_copy(x_hbm.at[pl.ds(0, bm), :], xbuf.at[slot], gsem.at[slot]).wait()

    def wait_scatters(slot):
        pltpu.make_async_copy(ybuf.at[slot], z_hbm.at[pl.ds(0, bm), :], ssem.at[slot]).wait()

    @pl.when((blk == 0) & (f == 0))
    def _():
        ybuf[1] = jnp.zeros((bm, d), f32)

        def start(j, _):
            gather_row(0, j, 0).start()
            return 0
        lax.fori_loop(0, bm, start, 0, unroll=ISSUE_UNROLL)

    @pl.when(f == 0)
    def _():
        wait_gathers(s)

        @pl.when(blk > 0)
        def _():
            wait_scatters(s)

        xb16[...] = xbuf[s, :, 0:d].astype(bf16)
        lane = lax.broadcasted_iota(i32, (bm, LANES), 1)
        vcol[...] = jnp.sum(jnp.where(lane == e, xbuf[s, :, d:], 0.0), axis=1, keepdims=True)

    nxt = jnp.minimum(blk + 1, last_blk)
    prv = jnp.maximum(blk - 1, 0)
    for step in range(nf_static):
        @pl.when(f == step)
        def _():
            for j in range(step * rows, (step + 1) * rows):
                gather_row(nxt, j, 1 - s).start()
                scatter_row(prv, j, 1 - s).start()

    x = xb16[...]
    gate = _dot(x, wg_ref[...])
    up = _dot(x, wu_ref[...])
    hid = (gate * _sigmoid(gate) * up).astype(bf16)

    if nf_static == 1:
        ybuf[s] = _dot(hid, wd_ref[...]) * vcol[...]
    else:
        @pl.when(f == 0)
        def _():
            ybuf[s] = _dot(hid, wd_ref[...])

        @pl.when((f > 0) & (f < nf - 1))
        def _():
            ybuf[s] += _dot(hid, wd_ref[...])

        @pl.when(f == nf - 1)
        def _():
            ybuf[s] = (ybuf[s] + _dot(hid, wd_ref[...])) * vcol[...]

    @pl.when((blk == last_blk) & (f == nf - 1))
    def _():
        wait_gathers(1 - s)
        wait_scatters(1 - s)

        def start(j, _):
            scatter_row(blk, j, s).start()
            return 0
        lax.fori_loop(0, bm, start, 0, unroll=ISSUE_UNROLL)
        wait_scatters(s)


def experts(idx, dst, xext, wg, wu, wd, layer, cap, bm, tf):
    _, n_exp, d, hidden = wg.shape
    grid_spec = pltpu.PrefetchScalarGridSpec(
        num_scalar_prefetch=2,
        grid=(n_exp, cap // bm, hidden // tf),
        in_specs=[
            pl.BlockSpec(memory_space=pl.ANY),
            pl.BlockSpec((None, None, d, tf), lambda e, m, f, i_, d_: (layer, e, 0, f)),
            pl.BlockSpec((None, None, d, tf), lambda e, m, f, i_, d_: (layer, e, 0, f)),
            pl.BlockSpec((None, None, tf, d), lambda e, m, f, i_, d_: (layer, e, f, 0)),
        ],
        out_specs=pl.BlockSpec(memory_space=pl.ANY),
        scratch_shapes=[
            pltpu.VMEM((2, bm, d + LANES), f32),
            pltpu.VMEM((bm, d), bf16),
            pltpu.VMEM((bm, 1), f32),
            pltpu.VMEM((2, bm, d), f32),
            pltpu.SemaphoreType.DMA((2,)),
            pltpu.SemaphoreType.DMA((2,)),
        ],
    )
    assert cap % bm == 0 and bm % (hidden // tf) == 0
    return pl.pallas_call(
        functools.partial(_expert_kernel, nf_static=hidden // tf),
        grid_spec=grid_spec,
        out_shape=jax.ShapeDtypeStruct((n_exp * cap, d), f32),
        compiler_params=pltpu.CompilerParams(dimension_semantics=("arbitrary",) * 3,
                                             vmem_limit_bytes=EXPERT_VMEM_LIMIT_BYTES),
        name="moe_experts",
    )(idx, dst, xext, wg, wu, wd)


def _combine_kernel(ts_ref, h_ref, aux_ref, eye_ref, z_hbm, o_ref, zbuf, sem, slot_ref, *, total):
    i = pl.program_id(0)
    nt = pl.num_programs(0)
    tm, d = h_ref.shape
    cw = zbuf.shape[1]
    sub = lax.broadcasted_iota(i32, (SUBLANES_BF16, cw), 0)
    rt = aux_ref[0, 0:1, :]
    kt = aux_ref[0, 1:2, :]
    parts = jnp.where(sub == 0, rt >> 8, jnp.where(sub == 1, rt & 255, jnp.where(sub == 2, kt, 0)))
    col = _dot_nt(eye_ref[...], parts.astype(f32).astype(bf16))
    rlo = col[:, 0:1] * 256.0 + col[:, 1:2]
    rhi = rlo + col[:, 2:3]
    def first_row(t):
        return (ts_ref[t] // SUBLANES_F32) * SUBLANES_F32

    def window(nominal):
        return pl.multiple_of(jnp.minimum(nominal, total - cw), SUBLANES_F32)

    def chunk_copy(nominal, slot):
        return pltpu.make_async_copy(z_hbm.at[pl.ds(window(nominal), cw), :], zbuf.at[slot], sem.at[slot])

    a0 = first_row(i)
    nch = jnp.maximum((ts_ref[i + 1] - a0 + cw - 1) // cw, 1)

    @pl.when(i == 0)
    def _():
        slot_ref[0] = 0
        chunk_copy(a0, 0).start()

    s0 = slot_ref[0]
    o_ref[...] = h_ref[...]
    lane = lax.broadcasted_iota(i32, (1, cw), 1)

    def chunk(k, _):
        slot = (s0 + k) % 2
        nominal = a0 + k * cw

        @pl.when(k + 1 < nch)
        def _():
            chunk_copy(nominal + cw, 1 - slot).start()

        @pl.when((k + 1 == nch) & (i + 1 < nt))
        def _():
            chunk_copy(first_row(i + 1), 1 - slot).start()
            slot_ref[0] = 1 - slot

        chunk_copy(nominal, slot).wait()
        row = window(nominal) + lane
        rowf = row.astype(f32)
        onehot = ((rowf >= rlo) & (rowf < rhi) & (row >= nominal)).astype(f32).astype(bf16)
        o_ref[...] += _dot(onehot, zbuf[slot].astype(bf16))
        return 0

    lax.fori_loop(0, nch, chunk, 0)


def combine(tstart, h, aux3, z):
    n, d = h.shape
    cw = aux3.shape[2]
    total = z.shape[0]
    eye = jnp.eye(cw, dtype=bf16)
    grid_spec = pltpu.PrefetchScalarGridSpec(
        num_scalar_prefetch=1,
        grid=(n // cw,),
        in_specs=[
            pl.BlockSpec((cw, d), lambda i, s: (i, 0)),
            pl.BlockSpec((1,) + aux3.shape[1:], lambda i, s: (i, 0, 0)),
            pl.BlockSpec(eye.shape, lambda i, s: (0, 0)),
            pl.BlockSpec(memory_space=pl.ANY),
        ],
        out_specs=pl.BlockSpec((cw, d), lambda i, s: (i, 0)),
        scratch_shapes=[pltpu.VMEM((2, cw, d), f32), pltpu.SemaphoreType.DMA((2,)), pltpu.SMEM((1,), i32)],
    )
    return pl.pallas_call(
        functools.partial(_combine_kernel, total=total),
        grid_spec=grid_spec,
        out_shape=jax.ShapeDtypeStruct((n, d), f32),
        compiler_params=_cp("arbitrary"),
        name="moe_combine",
    )(tstart, h, aux3, eye, z)


def moe(h, g, wr_hi, wr_lo, b_r, wg, wu, wd, layer, n_exp, tm, bm, tf):
    n, d = h.shape
    cap = CAPACITY_FACTOR * n // n_exp
    xext, aff3 = router(h, g, wr_hi, wr_lo, b_r, n_exp, tm)
    pos3, rank3, aux3, cst3 = select(aff3, cap)
    cst_flat = jnp.concatenate([cst3[:, :, 0].T, jnp.full((n_exp, 1), cap, i32)], axis=1).reshape(-1)
    idx, dst = compact(cst_flat, pos3, rank3, cap)
    z = experts(idx.reshape(-1), dst.reshape(-1), xext, wg, wu, wd, layer, cap, bm, tf)
    tstart = jnp.concatenate([aux3[:, 2, 0], jnp.full((1,), n_exp * cap, i32)])
    return combine(tstart, h, aux3, z)


def _ple_kernel(h_ref, p_ref, g_ref, wg_ref, wp_ref, gf_ref, o_ref, *, final):
    h = h_ref[...]
    gate = _sigmoid(_dot(_rms(h, g_ref[...]).astype(bf16), wg_ref[...]))
    out = h + gate * _dot(p_ref[...].astype(bf16), wp_ref[...])
    if final:
        out = _rms(out, gf_ref[...])
    o_ref[...] = out


def ple(h, p, g, w_gate, w_proj, g_final, final, tm):
    n, d = h.shape
    pd = p.shape[1]
    return pl.pallas_call(
        functools.partial(_ple_kernel, final=final),
        grid=(n // tm,),
        in_specs=[
            pl.BlockSpec((tm, d), lambda i: (i, 0)),
            pl.BlockSpec((tm, pd), lambda i: (i, 0)),
            pl.BlockSpec((1, d), lambda i: (0, 0)),
            pl.BlockSpec((d, d), lambda i: (0, 0)),
            pl.BlockSpec((pd, d), lambda i: (0, 0)),
            pl.BlockSpec((1, d), lambda i: (0, 0)),
        ],
        out_specs=pl.BlockSpec((tm, d), lambda i: (i, 0)),
        out_shape=jax.ShapeDtypeStruct((n, d), f32),
        compiler_params=_cp("parallel"),
        name="ple_final" if final else "ple",
    )(h, p, g, w_gate, w_proj, g_final)


def _tiles(n, seq, cap):
    tm = min(512, n)
    ts = min(256, seq)
    bm = min(512, cap)
    return tm, ts, bm


def _run_trunk(x, p, wts, n_exp):
    bsz, seq, d = x.shape
    depth = p.shape[0]
    n = bsz * seq
    cap = CAPACITY_FACTOR * n // n_exp
    tm, ts, bm = _tiles(n, seq, cap)
    hidden = wts["w_gate"].shape[-1]
    tf = min(1024, hidden)
    inner = wts["w_c_out"].shape[1]
    tn = min(1024, inner)
    aw = wts["w_a_mix"].shape[1] * wts["w_a_mix"].shape[2]
    ch = wts["w_a_mix"].shape[2]
    s2 = DFT_INNER
    s1 = seq // s2
    tb = min(s2, max(SUBLANES_F32, DFT_ROWS // s1))
    k1, t2, cs = _dft_tables(seq, tb, ch)
    cb = min(512, aw)

    h = x.reshape(n, d)
    for i in range(depth):
        j = i // 2
        if i % 2 == 0:
            ua, ub = even_in(h, wts["g_mix"][i], wts["w_ab_in"][j], tm)
            a5 = dft_stage1(ua.reshape(bsz, s1, s2, aw), k1, tb, cb)
            yp = dft_stage2(a5, t2, cs, wts["w_a_mix"][j])
            ya = jnp.swapaxes(yp, 1, 2).reshape(bsz, seq, aw)
            h = even_out(ub.reshape(bsz, seq, -1), ya, h.reshape(bsz, seq, d), wts["w_b_mix"][j],
                         wts["b_scale"][j], wts["w_ab_out"][j], ts).reshape(n, d)
        else:
            v = odd_in(h, wts["g_mix"][i], wts["w_c_in"][j], tm, tn)
            h = odd_out(v.reshape(bsz, seq, inner), wts["c_dw"][j], wts["c_ln_g"][j], wts["c_ln_b"][j],
                        wts["w_c_out"][j], h.reshape(bsz, seq, d), ts).reshape(n, d)
        h = moe(h, wts["g_ffn"][i], wts["w_router_hi"][i], wts["w_router_lo"][i], wts["b_router"][i],
                wts["w_gate"], wts["w_up"], wts["w_down"], i, n_exp, tm, bm, tf)
        h = ple(h, p[i].reshape(n, -1), wts["g_ple"][i], wts["w_ple_gate"][i], wts["w_ple_proj"][i],
                wts["g_final"], i == depth - 1, tm)
    return h.reshape(bsz, seq, d)


def kernel(x_prompt, x_sample, p_prompt, p_sample, g_mix, w_ab_in, w_a_mix, w_b_mix, b_scale, w_ab_out,
           w_c_in, c_dw, c_ln_g, c_ln_b, w_c_out, g_ffn, w_router, b_router, w_gate, w_up, w_down,
           g_ple, w_ple_gate, w_ple_proj, g_final):
    depth, d = g_mix.shape
    n_exp = w_router.shape[-1]
    wr = jnp.pad(w_router, ((0, 0), (0, 0), (0, LANES - n_exp)))
    wr_hi = wr.astype(bf16)
    wr_lo = (wr - wr_hi.astype(f32)).astype(bf16)
    wts = dict(
        g_mix=g_mix[:, None, :],
        w_ab_in=w_ab_in.astype(bf16),
        w_a_mix=w_a_mix.astype(bf16),
        w_b_mix=w_b_mix.astype(bf16),
        b_scale=b_scale[:, None, :],
        w_ab_out=w_ab_out.astype(bf16),
        w_c_in=w_c_in.astype(bf16),
        c_dw=c_dw,
        c_ln_g=c_ln_g[:, None, :],
        c_ln_b=c_ln_b[:, None, :],
        w_c_out=w_c_out.astype(bf16),
        g_ffn=g_ffn[:, None, :],
        w_router_hi=wr_hi,
        w_router_lo=wr_lo,
        b_router=jnp.pad(b_router, ((0, 0), (0, LANES - n_exp)))[:, None, :],
        w_gate=w_gate.astype(bf16),
        w_up=w_up.astype(bf16),
        w_down=w_down.astype(bf16),
        g_ple=g_ple[:, None, :],
        w_ple_gate=w_ple_gate.astype(bf16),
        w_ple_proj=w_ple_proj.astype(bf16),
        g_final=g_final[None, :],
    )
    y_prompt = _run_trunk(x_prompt, p_prompt, wts, n_exp)
    y_sample = _run_trunk(x_sample, p_sample, wts, n_exp)
    return (y_prompt, y_sample)
```

```python
import functools

import numpy as np
import jax
import jax.numpy as jnp
from jax import lax
from jax.experimental import pallas as pl
from jax.experimental.pallas import tpu as pltpu

f32 = jnp.float32
bf16 = jnp.bfloat16
i32 = jnp.int32

EPS = 1e-6
CAPACITY_FACTOR = 2
N_FOURIER_GROUPS = 4
POOL_WINDOWS = (2, 4, 8, 16)

LANES = 128
SUBLANES_F32 = 8
SUBLANES_BF16 = 16
MXU_DIM = 256
VMEM_LIMIT_BYTES = 56 * 1024 * 1024
EXPERT_VMEM_LIMIT_BYTES = 60 * 1024 * 1024

ROUTE_CHUNK = MXU_DIM
DFT_INNER = 128
DFT_ROWS = 1024
HALO = 16
ISSUE_UNROLL = 8


def _cp(*dims):
    return pltpu.CompilerParams(dimension_semantics=dims, vmem_limit_bytes=VMEM_LIMIT_BYTES)


def _rms(x, g):
    ms = jnp.mean(x * x, axis=-1, keepdims=True)
    return x * lax.rsqrt(ms + EPS) * g


def _sigmoid(x):
    return 1.0 / (1.0 + jnp.exp(-x))


def _dot(a, b):
    return jnp.dot(a, b, preferred_element_type=f32)


def _dot_nt(a, b):
    return lax.dot_general(a, b, (((1,), (1,)), ((), ())), preferred_element_type=f32)


def _even_in_kernel(h_ref, g_ref, w_ref, ua_ref, ub_ref):
    y = _rms(h_ref[...], g_ref[...]).astype(bf16)
    u = _dot(y, w_ref[...])
    half = ua_ref.shape[-1]
    ua_ref[...] = u[:, :half]
    ub_ref[...] = u[:, half:].astype(bf16)


def even_in(h, g, w, tm):
    n, d = h.shape
    mix = w.shape[1]
    half = mix // 2
    return pl.pallas_call(
        _even_in_kernel,
        grid=(n // tm,),
        in_specs=[
            pl.BlockSpec((tm, d), lambda i: (i, 0)),
            pl.BlockSpec((1, d), lambda i: (0, 0)),
            pl.BlockSpec((d, mix), lambda i: (0, 0)),
        ],
        out_specs=[
            pl.BlockSpec((tm, half), lambda i: (i, 0)),
            pl.BlockSpec((tm, half), lambda i: (i, 0)),
        ],
        out_shape=[jax.ShapeDtypeStruct((n, half), f32), jax.ShapeDtypeStruct((n, half), bf16)],
        compiler_params=_cp("parallel"),
        name="even_in",
    )(h, g, w)


def _dft_tables(seq, tb, ch):
    s2 = DFT_INNER
    s1 = seq // s2
    a = jnp.arange(s1, dtype=i32)
    ang1 = (2.0 * np.pi / s1) * ((a[:, None] * a[None, :]) % s1).astype(f32)
    f1 = jnp.concatenate([jnp.cos(ang1), -jnp.sin(ang1)], axis=0) * (1.0 / np.sqrt(s1))
    k1 = jnp.kron(f1, jnp.eye(tb, dtype=f32)).astype(bf16)
    b = jnp.arange(s2, dtype=i32)
    k = (b[None, None, :] * (b[None, :, None] * s1 + a[:, None, None])) % seq
    phi = (2.0 * np.pi / seq) * k.astype(f32)
    mr = jnp.cos(phi) * (1.0 / np.sqrt(s2))
    mi = -jnp.sin(phi) * (1.0 / np.sqrt(s2))
    t2 = jnp.concatenate(
        [jnp.concatenate([mr, -mi], axis=2), jnp.concatenate([mi, mr], axis=2)], axis=1
    ).astype(bf16)
    c = jnp.arange(ch, dtype=i32)
    th = (2.0 * np.pi / ch) * ((c[:, None] * c[None, :]) % ch).astype(f32)
    cs = (jnp.concatenate([jnp.cos(th), jnp.sin(th)], axis=0) * (1.0 / np.sqrt(ch))).astype(bf16)
    return k1, t2, cs


def _dft1_kernel(x_ref, k_ref, o_ref):
    _, s1, tb, cb = x_ref.shape
    x = x_ref[0].reshape(s1 * tb, cb).astype(bf16)
    r = _dot(k_ref[...], x)
    o_ref[0] = r.reshape(2, s1, tb, cb)


def dft_stage1(ua4, k1, tb, cb):
    bsz, s1, s2, c = ua4.shape
    return pl.pallas_call(
        _dft1_kernel,
        grid=(bsz, s2 // tb, c // cb),
        in_specs=[
            pl.BlockSpec((1, s1, tb, cb), lambda b, j, k: (b, 0, j, k)),
            pl.BlockSpec(k1.shape, lambda b, j, k: (0, 0)),
        ],
        out_specs=pl.BlockSpec((1, 2, s1, tb, cb), lambda b, j, k: (b, 0, 0, j, k)),
        out_shape=jax.ShapeDtypeStruct((bsz, 2, s1, s2, c), f32),
        compiler_params=_cp("parallel", "parallel", "parallel"),
        name="dft_stage1",
    )(ua4, k1)


def _dft2_kernel(a_ref, t_ref, cs_ref, w_ref, o_ref):
    s2, c = a_ref.shape[3], a_ref.shape[4]
    groups, ch, _ = w_ref.shape
    a = a_ref[0, :, 0].reshape(2 * s2, c).astype(bf16)
    p = _dot(t_ref[0], a)
    for g in range(groups):
        cols = slice(g * ch, (g + 1) * ch)
        pre = p[:s2, cols].astype(bf16)
        pim = p[s2:, cols].astype(bf16)
        f = _dot(pre, cs_ref[:ch, :]) + _dot(pim, cs_ref[ch:, :])
        y = _dot(f.astype(bf16), w_ref[g])
        o_ref[0, 0, :, cols] = y.astype(o_ref.dtype)


def dft_stage2(a5, t2, cs, w_a):
    bsz, _, s1, s2, c = a5.shape
    return pl.pallas_call(
        _dft2_kernel,
        grid=(bsz, s1),
        in_specs=[
            pl.BlockSpec((1, 2, 1, s2, c), lambda b, j: (b, 0, j, 0, 0)),
            pl.BlockSpec((1,) + t2.shape[1:], lambda b, j: (j, 0, 0)),
            pl.BlockSpec(cs.shape, lambda b, j: (0, 0)),
            pl.BlockSpec(w_a.shape, lambda b, j: (0, 0, 0)),
        ],
        out_specs=pl.BlockSpec((1, 1, s2, c), lambda b, j: (b, j, 0, 0)),
        out_shape=jax.ShapeDtypeStruct((bsz, s1, s2, c), bf16),
        compiler_params=_cp("parallel", "parallel"),
        name="dft_stage2",
    )(a5, t2, cs, w_a)


def _even_out_kernel(ub_ref, ubp_ref, ubn_ref, ya_ref, h_ref, wb_ref, bs_ref, wo_ref, o_ref, xs_ref,
                     *, seq):
    i = pl.program_id(1)
    nt = pl.num_programs(1)
    ts = ub_ref.shape[1]
    groups, ch, _ = wb_ref.shape
    aw = ya_ref.shape[2]
    xs_ref[0:HALO, :] = jnp.where(i > 0, ubp_ref[0].astype(f32), 0.0)
    xs_ref[HALO:HALO + ts, :] = ub_ref[0].astype(f32)
    xs_ref[HALO + ts:, :] = jnp.where(i < nt - 1, ubn_ref[0].astype(f32), 0.0)
    t = i * ts + lax.broadcasted_iota(i32, (ts, 1), 0)
    acc = h_ref[0] + _dot(ya_ref[0], wo_ref[0:aw, :])
    for g in range(groups):
        w = POOL_WINDOWS[g]
        cols = pl.ds(g * ch, ch)
        s = xs_ref[pl.ds(HALO - w // 2, ts), cols]
        for o in range(-w // 2 + 1, w // 2):
            s = s + xs_ref[pl.ds(HALO + o, ts), cols]
        cnt = (jnp.minimum(t + w // 2, seq) - jnp.maximum(t - w // 2, 0)).astype(f32)
        pooled = s / cnt - xs_ref[pl.ds(HALO, ts), cols]
        yb = _dot(pooled.astype(bf16), wb_ref[g]) * bs_ref[:, cols]
        acc = acc + _dot(yb.astype(bf16), wo_ref[pl.ds(aw + g * ch, ch), :])
    o_ref[0] = acc


def even_out(ub3, ya3, h3, w_b, b_scale, w_out, ts):
    bsz, seq, bw = ub3.shape
    d = h3.shape[2]
    aw = ya3.shape[2]
    nh = ts // HALO
    last = seq // HALO - 1
    return pl.pallas_call(
        functools.partial(_even_out_kernel, seq=seq),
        grid=(bsz, seq // ts),
        in_specs=[
            pl.BlockSpec((1, ts, bw), lambda b, i: (b, i, 0)),
            pl.BlockSpec((1, HALO, bw), lambda b, i: (b, jnp.maximum(i * nh - 1, 0), 0)),
            pl.BlockSpec((1, HALO, bw), lambda b, i: (b, jnp.minimum((i + 1) * nh, last), 0)),
            pl.BlockSpec((1, ts, aw), lambda b, i: (b, i, 0)),
            pl.BlockSpec((1, ts, d), lambda b, i: (b, i, 0)),
            pl.BlockSpec(w_b.shape, lambda b, i: (0, 0, 0)),
            pl.BlockSpec((1, bw), lambda b, i: (0, 0)),
            pl.BlockSpec(w_out.shape, lambda b, i: (0, 0)),
        ],
        out_specs=pl.BlockSpec((1, ts, d), lambda b, i: (b, i, 0)),
        out_shape=jax.ShapeDtypeStruct((bsz, seq, d), f32),
        scratch_shapes=[pltpu.VMEM((ts + 2 * HALO, bw), f32)],
        compiler_params=_cp("parallel", "parallel"),
        name="even_out",
    )(ub3, ub3, ub3, ya3, h3, w_b, b_scale, w_out)


def _odd_in_kernel(h_ref, g_ref, wa_ref, wb_ref, v_ref):
    y = _rms(h_ref[...], g_ref[...]).astype(bf16)
    a = _dot(y, wa_ref[...])
    b = _dot(y, wb_ref[...])
    v_ref[...] = a * _sigmoid(b)


def odd_in(h, g, w, tm, tn):
    n, d = h.shape
    inner = w.shape[1] // 2
    ncol = inner // tn
    return pl.pallas_call(
        _odd_in_kernel,
        grid=(ncol, n // tm),
        in_specs=[
            pl.BlockSpec((tm, d), lambda j, i: (i, 0)),
            pl.BlockSpec((1, d), lambda j, i: (0, 0)),
            pl.BlockSpec((d, tn), lambda j, i: (0, j)),
            pl.BlockSpec((d, tn), lambda j, i: (0, j + ncol)),
        ],
        out_specs=pl.BlockSpec((tm, tn), lambda j, i: (i, j)),
        out_shape=jax.ShapeDtypeStruct((n, inner), f32),
        compiler_params=_cp("parallel", "parallel"),
        name="odd_in",
    )(h, g, w, w)


CONV_ROWS = 32
CONV_COLS = 512


def _odd_out_kernel(v_ref, vp_ref, vn_ref, dw_ref, lg_ref, lb_ref, wo_ref, h_ref, o_ref, xs_ref, sh_ref,
                    cv_ref, *, taps):
    i = pl.program_id(1)
    nt = pl.num_programs(1)
    ts, inner = v_ref.shape[1], v_ref.shape[2]
    first = HALO - taps // 2
    span = sh_ref.shape[1]
    xs_ref[0:HALO, :] = jnp.where(i > 0, vp_ref[0], 0.0)
    xs_ref[HALO:HALO + ts, :] = v_ref[0]
    xs_ref[HALO + ts:, :] = jnp.where(i < nt - 1, vn_ref[0], 0.0)
    for c0 in range(0, inner, CONV_COLS):
        cols = pl.ds(c0, CONV_COLS)
        for r in range(1, SUBLANES_F32):
            sh_ref[r] = xs_ref[pl.ds(r, span), cols]
        for r0 in range(0, ts, CONV_ROWS):
            acc = jnp.zeros((CONV_ROWS, CONV_COLS), f32)
            for k in range(taps):
                q, r = divmod(first + k, SUBLANES_F32)
                rows = pl.ds(r0 + q * SUBLANES_F32, CONV_ROWS)
                x = xs_ref[rows, cols] if r == 0 else sh_ref[r, rows, :]
                acc = acc + dw_ref[k:k + 1, cols] * x
            cv_ref[pl.ds(r0, CONV_ROWS), cols] = acc
    cv = cv_ref[...]
    mu = jnp.mean(cv, axis=-1, keepdims=True)
    xc = cv - mu
    var = jnp.mean(xc * xc, axis=-1, keepdims=True)
    y = xc * lax.rsqrt(var + EPS) * lg_ref[...] + lb_ref[...]
    y = y * _sigmoid(y)
    o_ref[0] = h_ref[0] + _dot(y.astype(bf16), wo_ref[...])


def odd_out(v3, dw, ln_g, ln_b, w_out, h3, ts):
    bsz, seq, inner = v3.shape
    d = h3.shape[2]
    taps = dw.shape[0]
    nh = ts // HALO
    last = seq // HALO - 1
    return pl.pallas_call(
        functools.partial(_odd_out_kernel, taps=taps),
        grid=(bsz, seq // ts),
        in_specs=[
            pl.BlockSpec((1, ts, inner), lambda b, i: (b, i, 0)),
            pl.BlockSpec((1, HALO, inner), lambda b, i: (b, jnp.maximum(i * nh - 1, 0), 0)),
            pl.BlockSpec((1, HALO, inner), lambda b, i: (b, jnp.minimum((i + 1) * nh, last), 0)),
            pl.BlockSpec(dw.shape, lambda b, i: (0, 0)),
            pl.BlockSpec((1, inner), lambda b, i: (0, 0)),
            pl.BlockSpec((1, inner), lambda b, i: (0, 0)),
            pl.BlockSpec(w_out.shape, lambda b, i: (0, 0)),
            pl.BlockSpec((1, ts, d), lambda b, i: (b, i, 0)),
        ],
        out_specs=pl.BlockSpec((1, ts, d), lambda b, i: (b, i, 0)),
        out_shape=jax.ShapeDtypeStruct((bsz, seq, d), f32),
        scratch_shapes=[
            pltpu.VMEM((ts + 2 * HALO, inner), f32),
            pltpu.VMEM((SUBLANES_F32, ts + SUBLANES_F32 * ((HALO + taps // 2) // SUBLANES_F32), CONV_COLS), f32),
            pltpu.VMEM((ts, inner), f32),
        ],
        compiler_params=_cp("parallel", "parallel"),
        name="odd_out",
    )(v3, v3, v3, dw, ln_g, ln_b, w_out, h3)


def _router_kernel(h_ref, g_ref, wh_ref, wl_ref, b_ref, x_ref, at_ref, *, n_exp):
    tm, d = h_ref.shape
    xn = _rms(h_ref[...], g_ref[...])
    xh = xn.astype(bf16)
    xl = (xn - xh.astype(f32)).astype(bf16)
    logits = _dot(xh, wh_ref[...]) + _dot(xl, wh_ref[...]) + _dot(xh, wl_ref[...]) + b_ref[...]
    lane = lax.broadcasted_iota(i32, logits.shape, 1)
    logits = jnp.where(lane < n_exp, logits, -1e30)
    m = jnp.max(logits, axis=-1, keepdims=True)
    p = jnp.exp(logits - m)
    aff = p / jnp.sum(p, axis=-1, keepdims=True)
    x_ref[:, 0:d] = xn
    x_ref[:, d:] = aff
    at = aff.T
    for q in range(tm // ROUTE_CHUNK):
        at_ref[q] = at[0:n_exp, q * ROUTE_CHUNK:(q + 1) * ROUTE_CHUNK]


def router(h, g, wh, wl, b, n_exp, tm):
    n, d = h.shape
    return pl.pallas_call(
        functools.partial(_router_kernel, n_exp=n_exp),
        grid=(n // tm,),
        in_specs=[
            pl.BlockSpec((tm, d), lambda i: (i, 0)),
            pl.BlockSpec((1, d), lambda i: (0, 0)),
            pl.BlockSpec((d, LANES), lambda i: (0, 0)),
            pl.BlockSpec((d, LANES), lambda i: (0, 0)),
            pl.BlockSpec((1, LANES), lambda i: (0, 0)),
        ],
        out_specs=[
            pl.BlockSpec((tm, d + LANES), lambda i: (i, 0)),
            pl.BlockSpec((tm // ROUTE_CHUNK, n_exp, ROUTE_CHUNK), lambda i: (i, 0, 0)),
        ],
        out_shape=[
            jax.ShapeDtypeStruct((n, d + LANES), f32),
            jax.ShapeDtypeStruct((n // ROUTE_CHUNK, n_exp, ROUTE_CHUNK), f32),
        ],
        compiler_params=_cp("parallel"),
        name="router",
    )(h, g, wh, wl, b)


def _select_kernel(a_ref, su_ref, sl_ref, pos_ref, rank_ref, aux_ref, cst_ref, bits_ref, *, cap):
    nchunk, n_exp, cw = a_ref.shape
    bits_ref[...] = pltpu.bitcast(a_ref[...], i32)

    def count_ge(cand):
        m = (bits_ref[...] >= cand[None]).astype(f32)
        return jnp.sum(jnp.sum(m, axis=0), axis=1, keepdims=True)

    def bisect(it, prefix):
        cand = prefix | jnp.left_shift(jnp.int32(1), 30 - it)
        return jnp.where(count_ge(cand) >= cap, cand, prefix)

    thr = lax.fori_loop(0, 31, bisect, jnp.zeros((n_exp, 1), i32))
    need = cap - (count_ge(thr + 1))

    sub = lax.broadcasted_iota(i32, (SUBLANES_F32, cw), 0)

    def chunk(c, carry):
        cg, cq = carry
        b = bits_ref[c]
        g = b > thr
        q = b == thr
        gq = jnp.concatenate([g.astype(f32), q.astype(f32)], axis=0).astype(bf16)
        pre = _dot(gq, su_ref[...])
        pg = cg + pre[:n_exp]
        pq = cq + pre[n_exp:]
        sel = g | (q & (pq < need))
        pos = pg + jnp.minimum(pq, need)
        self32 = sel.astype(f32)
        pos_ref[c] = jnp.where(sel, pos, -1.0).astype(i32)
        rtok = jnp.sum(pos, axis=0, keepdims=True)
        ktok = jnp.sum(self32, axis=0, keepdims=True)
        er = _dot(sl_ref[...], self32.astype(bf16))
        rank_ref[c] = (rtok + er).astype(i32)
        cstart = cg + jnp.minimum(cq, need)
        rstart = jnp.sum(cstart, axis=0, keepdims=True)
        aux = jnp.where(sub == 0, rtok, jnp.where(sub == 1, ktok, jnp.where(sub == 2, rstart, 0.0)))
        aux_ref[c] = aux.astype(i32)
        cst_ref[c] = jnp.broadcast_to(cstart, (n_exp, LANES)).astype(i32)
        return (cg + jnp.sum(g.astype(f32), axis=1, keepdims=True),
                cq + jnp.sum(q.astype(f32), axis=1, keepdims=True))

    zero = jnp.zeros((n_exp, 1), f32)
    lax.fori_loop(0, nchunk, chunk, (zero, zero))


def select(aff3, cap):
    nchunk, n_exp, cw = aff3.shape
    r = lax.broadcasted_iota(i32, (cw, cw), 0)
    c = lax.broadcasted_iota(i32, (cw, cw), 1)
    su = (r < c).astype(bf16)
    re = lax.broadcasted_iota(i32, (n_exp, n_exp), 0)
    ce = lax.broadcasted_iota(i32, (n_exp, n_exp), 1)
    sl = (ce < re).astype(bf16)
    full3 = lambda shape: pl.BlockSpec(shape, lambda i: (0, 0, 0))
    return pl.pallas_call(
        functools.partial(_select_kernel, cap=cap),
        grid=(1,),
        in_specs=[full3(aff3.shape), pl.BlockSpec(su.shape, lambda i: (0, 0)),
                  pl.BlockSpec(sl.shape, lambda i: (0, 0))],
        out_specs=[full3((nchunk, n_exp, cw)), full3((nchunk, n_exp, cw)),
                   full3((nchunk, SUBLANES_F32, cw)), full3((nchunk, n_exp, LANES))],
        out_shape=[
            jax.ShapeDtypeStruct((nchunk, n_exp, cw), i32),
            jax.ShapeDtypeStruct((nchunk, n_exp, cw), i32),
            jax.ShapeDtypeStruct((nchunk, SUBLANES_F32, cw), i32),
            jax.ShapeDtypeStruct((nchunk, n_exp, LANES), i32),
        ],
        scratch_shapes=[pltpu.VMEM((nchunk, n_exp, cw), i32)],
        compiler_params=_cp("arbitrary"),
        name="moe_select",
    )(aff3, su, sl)


def _compact_kernel(cst_ref, pos_ref, rank_ref, idx_ref, dst_ref, acc_ref):
    e = pl.program_id(0)
    nchunk, _, cw = pos_ref.shape
    nblk = idx_ref.shape[1]
    acc_ref[...] = jnp.zeros(acc_ref.shape, f32)
    sub = lax.broadcasted_iota(i32, (SUBLANES_BF16, cw), 0)
    slot = lax.broadcasted_iota(i32, (cw, cw), 0)
    lane = lax.broadcasted_iota(i32, (1, cw), 1)

    def chunk(c, _):
        first = cst_ref[e * (nchunk + 1) + c]
        b0 = first // cw
        p = pos_ref[c, pl.ds(e, 1), :]
        rk = rank_ref[c, pl.ds(e, 1), :]
        tok = c * cw + lane
        vals = jnp.where(sub == 0, tok >> 8, jnp.where(sub == 1, tok & 255,
               jnp.where(sub == 2, rk >> 8, jnp.where(sub == 3, rk & 255, 0))))
        vt = vals.astype(f32).astype(bf16)

        for b in (b0, b0 + 1):
            onehot = (p - b * cw == slot).astype(f32).astype(bf16)
            acc_ref[b] += _dot_nt(vt, onehot)
        return 0

    lax.fori_loop(0, nchunk, chunk, 0, unroll=4)
    for b in range(nblk):
        a = acc_ref[b]
        idx_ref[0, b:b + 1, :] = (a[0:1, :] * 256.0 + a[1:2, :]).astype(i32)
        dst_ref[0, b:b + 1, :] = (a[2:3, :] * 256.0 + a[3:4, :]).astype(i32)


def compact(cst_flat, pos3, rank3, cap):
    nchunk, n_exp, cw = pos3.shape
    nblk = cap // cw
    grid_spec = pltpu.PrefetchScalarGridSpec(
        num_scalar_prefetch=1,
        grid=(n_exp,),
        in_specs=[pl.BlockSpec(pos3.shape, lambda e, s: (0, 0, 0)),
                  pl.BlockSpec(rank3.shape, lambda e, s: (0, 0, 0))],
        out_specs=[pl.BlockSpec((1, nblk, cw), lambda e, s: (e, 0, 0)),
                   pl.BlockSpec((1, nblk, cw), lambda e, s: (e, 0, 0))],
        scratch_shapes=[pltpu.VMEM((nblk + 1, SUBLANES_BF16, cw), f32)],
    )
    return pl.pallas_call(
        _compact_kernel,
        grid_spec=grid_spec,
        out_shape=[jax.ShapeDtypeStruct((n_exp, nblk, cw), i32),
                   jax.ShapeDtypeStruct((n_exp, nblk, cw), i32)],
        compiler_params=_cp("arbitrary"),
        name="moe_compact",
    )(cst_flat, pos3, rank3)


def _expert_kernel(idx_ref, dst_ref, x_hbm, wg_ref, wu_ref, wd_ref, z_hbm,
                   xbuf, xb16, vcol, ybuf, gsem, ssem, *, nf_static):
    e = pl.program_id(0)
    m = pl.program_id(1)
    f = pl.program_id(2)
    nm = pl.num_programs(1)
    nf = pl.num_programs(2)
    _, bm, d = ybuf.shape
    blk = e * nm + m
    last_blk = pl.num_programs(0) * nm - 1
    s = blk % 2
    rows = bm // nf_static

    def gather_row(b, j, slot):
        return pltpu.make_async_copy(x_hbm.at[pl.ds(idx_ref[b * bm + j], 1), :],
                                     xbuf.at[slot, pl.ds(j, 1), :], gsem.at[slot])

    def scatter_row(b, j, slot):
        return pltpu.make_async_copy(ybuf.at[slot, pl.ds(j, 1), :],
                                     z_hbm.at[pl.ds(dst_ref[b * bm + j], 1), :], ssem.at[slot])

    def wait_gathers(slot):
        pltpu.make_async_copy(x_hbm.at[pl.ds(0, bm), :], xbuf.at[slot], gsem.at[slot]).wait()

    def wait_scatters(slot):
        pltpu.make_async_copy(ybuf.at[slot], z_hbm.at[pl.ds(0, bm), :], ssem.at[slot]).wait()

    @pl.when((blk == 0) & (f == 0))
    def _():
        ybuf[1] = jnp.zeros((bm, d), f32)

        def start(j, _):
            gather_row(0, j, 0).start()
            return 0
        lax.fori_loop(0, bm, start, 0, unroll=ISSUE_UNROLL)

    @pl.when(f == 0)
    def _():
        wait_gathers(s)

        @pl.when(blk > 0)
        def _():
            wait_scatters(s)

        xb16[...] = xbuf[s, :, 0:d].astype(bf16)
        lane = lax.broadcasted_iota(i32, (bm, LANES), 1)
        vcol[...] = jnp.sum(jnp.where(lane == e, xbuf[s, :, d:], 0.0), axis=1, keepdims=True)

    nxt = jnp.minimum(blk + 1, last_blk)
    prv = jnp.maximum(blk - 1, 0)
    for step in range(nf_static):
        @pl.when(f == step)
        def _():
            for j in range(step * rows, (step + 1) * rows):
                gather_row(nxt, j, 1 - s).start()
                scatter_row(prv, j, 1 - s).start()

    x = xb16[...]
    gate = _dot(x, wg_ref[...])
    up = _dot(x, wu_ref[...])
    hid = (gate * _sigmoid(gate) * up).astype(bf16)

    if nf_static == 1:
        ybuf[s] = _dot(hid, wd_ref[...]) * vcol[...]
    else:
        @pl.when(f == 0)
        def _():
            ybuf[s] = _dot(hid, wd_ref[...])

        @pl.when((f > 0) & (f < nf - 1))
        def _():
            ybuf[s] += _dot(hid, wd_ref[...])

        @pl.when(f == nf - 1)
        def _():
            ybuf[s] = (ybuf[s] + _dot(hid, wd_ref[...])) * vcol[...]

    @pl.when((blk == last_blk) & (f == nf - 1))
    def _():
        wait_gathers(1 - s)
        wait_scatters(1 - s)

        def start(j, _):
            scatter_row(blk, j, s).start()
            return 0
        lax.fori_loop(0, bm, start, 0, unroll=ISSUE_UNROLL)
        wait_scatters(s)


def experts(idx, dst, xext, wg, wu, wd, layer, cap, bm, tf):
    _, n_exp, d, hidden = wg.shape
    grid_spec = pltpu.PrefetchScalarGridSpec(
        num_scalar_prefetch=2,
        grid=(n_exp, cap // bm, hidden // tf),
        in_specs=[
            pl.BlockSpec(memory_space=pl.ANY),
            pl.BlockSpec((None, None, d, tf), lambda e, m, f, i_, d_: (layer, e, 0, f)),
            pl.BlockSpec((None, None, d, tf), lambda e, m, f, i_, d_: (layer, e, 0, f)),
            pl.BlockSpec((None, None, tf, d), lambda e, m, f, i_, d_: (layer, e, f, 0)),
        ],
        out_specs=pl.BlockSpec(memory_space=pl.ANY),
        scratch_shapes=[
            pltpu.VMEM((2, bm, d + LANES), f32),
            pltpu.VMEM((bm, d), bf16),
            pltpu.VMEM((bm, 1), f32),
            pltpu.VMEM((2, bm, d), f32),
            pltpu.SemaphoreType.DMA((2,)),
            pltpu.SemaphoreType.DMA((2,)),
        ],
    )
    assert cap % bm == 0 and bm % (hidden // tf) == 0
    return pl.pallas_call(
        functools.partial(_expert_kernel, nf_static=hidden // tf),
        grid_spec=grid_spec,
        out_shape=jax.ShapeDtypeStruct((n_exp * cap, d), f32),
        compiler_params=pltpu.CompilerParams(dimension_semantics=("arbitrary",) * 3,
                                             vmem_limit_bytes=EXPERT_VMEM_LIMIT_BYTES),
        name="moe_experts",
    )(idx, dst, xext, wg, wu, wd)


def _combine_kernel(ts_ref, h_ref, aux_ref, eye_ref, z_hbm, p_ref, g_ref, wg_ref, wp_ref, gf_ref, o_ref,
                    zbuf, sem, slot_ref, *, total, final):
    i = pl.program_id(0)
    nt = pl.num_programs(0)
    tm, d = h_ref.shape
    cw = zbuf.shape[1]
    cpt = tm // cw
    sub = lax.broadcasted_iota(i32, (SUBLANES_BF16, cw), 0)
    cols = []
    for q in range(cpt):
        rt = aux_ref[q, 0:1, :]
        kt = aux_ref[q, 1:2, :]
        parts = jnp.where(sub == 0, rt >> 8, jnp.where(sub == 1, rt & 255, jnp.where(sub == 2, kt, 0)))
        cols.append(_dot_nt(eye_ref[...], parts.astype(f32).astype(bf16)))
    col = jnp.concatenate(cols, axis=0) if cpt > 1 else cols[0]
    rlo = col[:, 0:1] * 256.0 + col[:, 1:2]
    rhi = rlo + col[:, 2:3]

    def first_row(t):
        return (ts_ref[t * cpt] // SUBLANES_F32) * SUBLANES_F32

    def window(nominal):
        return pl.multiple_of(jnp.minimum(nominal, total - cw), SUBLANES_F32)

    def chunk_copy(nominal, slot):
        return pltpu.make_async_copy(z_hbm.at[pl.ds(window(nominal), cw), :], zbuf.at[slot], sem.at[slot])

    a0 = first_row(i)
    nch = jnp.maximum((ts_ref[(i + 1) * cpt] - a0 + cw - 1) // cw, 1)

    @pl.when(i == 0)
    def _():
        slot_ref[0] = 0
        chunk_copy(a0, 0).start()

    s0 = slot_ref[0]
    o_ref[...] = h_ref[...]
    lane = lax.broadcasted_iota(i32, (1, cw), 1)

    def chunk(k, _):
        slot = (s0 + k) % 2
        nominal = a0 + k * cw

        @pl.when(k + 1 < nch)
        def _():
            chunk_copy(nominal + cw, 1 - slot).start()

        @pl.when((k + 1 == nch) & (i + 1 < nt))
        def _():
            chunk_copy(first_row(i + 1), 1 - slot).start()
            slot_ref[0] = 1 - slot

        chunk_copy(nominal, slot).wait()
        row = window(nominal) + lane
        rowf = row.astype(f32)
        onehot = ((rowf >= rlo) & (rowf < rhi) & (row >= nominal)).astype(f32).astype(bf16)
        o_ref[...] += _dot(onehot, zbuf[slot].astype(bf16))
        return 0

    lax.fori_loop(0, nch, chunk, 0)

    hc = o_ref[...]
    gate = _sigmoid(_dot(_rms(hc, g_ref[...]).astype(bf16), wg_ref[...]))
    out = hc + gate * _dot(p_ref[...].astype(bf16), wp_ref[...])
    if final:
        out = _rms(out, gf_ref[...])
    o_ref[...] = out


def combine_ple(tstart, h, aux3, z, p, g, w_gate, w_proj, g_final, final, tm):
    n, d = h.shape
    cw = aux3.shape[2]
    pd = p.shape[1]
    total = z.shape[0]
    eye = jnp.eye(cw, dtype=bf16)
    grid_spec = pltpu.PrefetchScalarGridSpec(
        num_scalar_prefetch=1,
        grid=(n // tm,),
        in_specs=[
            pl.BlockSpec((tm, d), lambda i, s: (i, 0)),
            pl.BlockSpec((tm // cw,) + aux3.shape[1:], lambda i, s: (i, 0, 0)),
            pl.BlockSpec(eye.shape, lambda i, s: (0, 0)),
            pl.BlockSpec(memory_space=pl.ANY),
            pl.BlockSpec((tm, pd), lambda i, s: (i, 0)),
            pl.BlockSpec((1, d), lambda i, s: (0, 0)),
            pl.BlockSpec((d, d), lambda i, s: (0, 0)),
            pl.BlockSpec((pd, d), lambda i, s: (0, 0)),
            pl.BlockSpec((1, d), lambda i, s: (0, 0)),
        ],
        out_specs=pl.BlockSpec((tm, d), lambda i, s: (i, 0)),
        scratch_shapes=[pltpu.VMEM((2, cw, d), f32), pltpu.SemaphoreType.DMA((2,)), pltpu.SMEM((1,), i32)],
    )
    return pl.pallas_call(
        functools.partial(_combine_kernel, total=total, final=final),
        grid_spec=grid_spec,
        out_shape=jax.ShapeDtypeStruct((n, d), f32),
        compiler_params=_cp("arbitrary"),
        name="combine_ple_final" if final else "combine_ple",
    )(tstart, h, aux3, eye, z, p, g, w_gate, w_proj, g_final)


def moe_ple(h, g, wr_hi, wr_lo, b_r, wg, wu, wd, layer, n_exp, tm, bm, tf, p, g_ple, w_pg, w_pp, g_final,
            final):
    n, d = h.shape
    cap = CAPACITY_FACTOR * n // n_exp
    xext, aff3 = router(h, g, wr_hi, wr_lo, b_r, n_exp, tm)
    pos3, rank3, aux3, cst3 = select(aff3, cap)
    cst_flat = jnp.concatenate([cst3[:, :, 0].T, jnp.full((n_exp, 1), cap, i32)], axis=1).reshape(-1)
    idx, dst = compact(cst_flat, pos3, rank3, cap)
    z = experts(idx.reshape(-1), dst.reshape(-1), xext, wg, wu, wd, layer, cap, bm, tf)
    tstart = jnp.concatenate([aux3[:, 2, 0], jnp.full((1,), n_exp * cap, i32)])
    return combine_ple(tstart, h, aux3, z, p, g_ple, w_pg, w_pp, g_final, final, tm)


def _tiles(n, seq, cap):
    tm = min(512, n)
    ts = min(256, seq)
    bm = min(512, cap)
    return tm, ts, bm


def _run_trunk(x, p, wts, n_exp):
    bsz, seq, d = x.shape
    depth = p.shape[0]
    n = bsz * seq
    cap = CAPACITY_FACTOR * n // n_exp
    tm, ts, bm = _tiles(n, seq, cap)
    hidden = wts["w_gate"].shape[-1]
    tf = min(1024, hidden)
    inner = wts["w_c_out"].shape[1]
    tn = min(1024, inner)
    aw = wts["w_a_mix"].shape[1] * wts["w_a_mix"].shape[2]
    ch = wts["w_a_mix"].shape[2]
    s2 = DFT_INNER
    s1 = seq // s2
    tb = min(s2, max(SUBLANES_F32, DFT_ROWS // s1))
    k1, t2, cs = _dft_tables(seq, tb, ch)
    cb = min(512, aw)

    h = x.reshape(n, d)
    for i in range(depth):
        j = i // 2
        if i % 2 == 0:
            ua, ub = even_in(h, wts["g_mix"][i], wts["w_ab_in"][j], tm)
            a5 = dft_stage1(ua.reshape(bsz, s1, s2, aw), k1, tb, cb)
            yp = dft_stage2(a5, t2, cs, wts["w_a_mix"][j])
            ya = jnp.swapaxes(yp, 1, 2).reshape(bsz, seq, aw)
            h = even_out(ub.reshape(bsz, seq, -1), ya, h.reshape(bsz, seq, d), wts["w_b_mix"][j],
                         wts["b_scale"][j], wts["w_ab_out"][j], ts).reshape(n, d)
        else:
            v = odd_in(h, wts["g_mix"][i], wts["w_c_in"][j], tm, tn)
            h = odd_out(v.reshape(bsz, seq, inner), wts["c_dw"][j], wts["c_ln_g"][j], wts["c_ln_b"][j],
                        wts["w_c_out"][j], h.reshape(bsz, seq, d), ts).reshape(n, d)
        h = moe_ple(h, wts["g_ffn"][i], wts["w_router_hi"][i], wts["w_router_lo"][i], wts["b_router"][i],
                    wts["w_gate"], wts["w_up"], wts["w_down"], i, n_exp, tm, bm, tf,
                    p[i].reshape(n, -1), wts["g_ple"][i], wts["w_ple_gate"][i], wts["w_ple_proj"][i],
                    wts["g_final"], i == depth - 1)
    return h.reshape(bsz, seq, d)


def kernel(x_prompt, x_sample, p_prompt, p_sample, g_mix, w_ab_in, w_a_mix, w_b_mix, b_scale, w_ab_out,
           w_c_in, c_dw, c_ln_g, c_ln_b, w_c_out, g_ffn, w_router, b_router, w_gate, w_up, w_down,
           g_ple, w_ple_gate, w_ple_proj, g_final):
    depth, d = g_mix.shape
    n_exp = w_router.shape[-1]
    wr = jnp.pad(w_router, ((0, 0), (0, 0), (0, LANES - n_exp)))
    wr_hi = wr.astype(bf16)
    wr_lo = (wr - wr_hi.astype(f32)).astype(bf16)
    wts = dict(
        g_mix=g_mix[:, None, :],
        w_ab_in=w_ab_in.astype(bf16),
        w_a_mix=w_a_mix.astype(bf16),
        w_b_mix=w_b_mix.astype(bf16),
        b_scale=b_scale[:, None, :],
        w_ab_out=w_ab_out.astype(bf16),
        w_c_in=w_c_in.astype(bf16),
        c_dw=c_dw,
        c_ln_g=c_ln_g[:, None, :],
        c_ln_b=c_ln_b[:, None, :],
        w_c_out=w_c_out.astype(bf16),
        g_ffn=g_ffn[:, None, :],
        w_router_hi=wr_hi,
        w_router_lo=wr_lo,
        b_router=jnp.pad(b_router, ((0, 0), (0, LANES - n_exp)))[:, None, :],
        w_gate=w_gate.astype(bf16),
        w_up=w_up.astype(bf16),
        w_down=w_down.astype(bf16),
        g_ple=g_ple[:, None, :],
        w_ple_gate=w_ple_gate.astype(bf16),
        w_ple_proj=w_ple_proj.astype(bf16),
        g_final=g_final[None, :],
    )
    y_prompt = _run_trunk(x_prompt, p_prompt, wts, n_exp)
    y_sample = _run_trunk(x_sample, p_sample, wts, n_exp)
    return (y_prompt, y_sample)
```

```python
import functools

import numpy as np
import jax
import jax.numpy as jnp
from jax import lax
from jax.experimental import pallas as pl
from jax.experimental.pallas import tpu as pltpu

f32 = jnp.float32
bf16 = jnp.bfloat16
i32 = jnp.int32

EPS = 1e-6
CAPACITY_FACTOR = 2
N_FOURIER_GROUPS = 4
POOL_WINDOWS = (2, 4, 8, 16)

LANES = 128
SUBLANES_F32 = 8
SUBLANES_BF16 = 16
MXU_DIM = 256
VMEM_LIMIT_BYTES = 56 * 1024 * 1024
EXPERT_VMEM_LIMIT_BYTES = 60 * 1024 * 1024

ROUTE_CHUNK = MXU_DIM
DFT_INNER = 128
DFT_ROWS = 1024
HALO = 16
ISSUE_UNROLL = 8


def _cp(*dims):
    return pltpu.CompilerParams(dimension_semantics=dims, vmem_limit_bytes=VMEM_LIMIT_BYTES)


def _rms(x, g):
    ms = jnp.mean(x * x, axis=-1, keepdims=True)
    return x * lax.rsqrt(ms + EPS) * g


def _sigmoid(x):
    return 1.0 / (1.0 + jnp.exp(-x))


def _dot(a, b):
    return jnp.dot(a, b, preferred_element_type=f32)


def _dot_nt(a, b):
    return lax.dot_general(a, b, (((1,), (1,)), ((), ())), preferred_element_type=f32)


def _even_in_kernel(h_ref, g_ref, w_ref, ua_ref, ub_ref):
    y = _rms(h_ref[...], g_ref[...]).astype(bf16)
    u = _dot(y, w_ref[...])
    half = ua_ref.shape[-1]
    ua_ref[...] = u[:, :half]
    ub_ref[...] = u[:, half:].astype(bf16)


def even_in(h, g, w, tm):
    n, d = h.shape
    mix = w.shape[1]
    half = mix // 2
    return pl.pallas_call(
        _even_in_kernel,
        grid=(n // tm,),
        in_specs=[
            pl.BlockSpec((tm, d), lambda i: (i, 0)),
            pl.BlockSpec((1, d), lambda i: (0, 0)),
            pl.BlockSpec((d, mix), lambda i: (0, 0)),
        ],
        out_specs=[
            pl.BlockSpec((tm, half), lambda i: (i, 0)),
            pl.BlockSpec((tm, half), lambda i: (i, 0)),
        ],
        out_shape=[jax.ShapeDtypeStruct((n, half), f32), jax.ShapeDtypeStruct((n, half), bf16)],
        compiler_params=_cp("parallel"),
        name="even_in",
    )(h, g, w)


def _dft_tables(seq, tb, ch):
    s2 = DFT_INNER
    s1 = seq // s2
    a = jnp.arange(s1, dtype=i32)
    ang1 = (2.0 * np.pi / s1) * ((a[:, None] * a[None, :]) % s1).astype(f32)
    f1 = jnp.concatenate([jnp.cos(ang1), -jnp.sin(ang1)], axis=0) * (1.0 / np.sqrt(s1))
    k1 = jnp.kron(f1, jnp.eye(tb, dtype=f32)).astype(bf16)
    b = jnp.arange(s2, dtype=i32)
    k = (b[None, None, :] * (b[None, :, None] * s1 + a[:, None, None])) % seq
    phi = (2.0 * np.pi / seq) * k.astype(f32)
    mr = jnp.cos(phi) * (1.0 / np.sqrt(s2))
    mi = -jnp.sin(phi) * (1.0 / np.sqrt(s2))
    t2 = jnp.concatenate(
        [jnp.concatenate([mr, -mi], axis=2), jnp.concatenate([mi, mr], axis=2)], axis=1
    ).astype(bf16)
    c = jnp.arange(ch, dtype=i32)
    th = (2.0 * np.pi / ch) * ((c[:, None] * c[None, :]) % ch).astype(f32)
    cs = (jnp.concatenate([jnp.cos(th), jnp.sin(th)], axis=0) * (1.0 / np.sqrt(ch))).astype(bf16)
    return k1, t2, cs


def _dft1_kernel(x_ref, k_ref, o_ref):
    _, s1, tb, cb = x_ref.shape
    x = x_ref[0].reshape(s1 * tb, cb).astype(bf16)
    r = _dot(k_ref[...], x)
    o_ref[0] = r.reshape(2, s1, tb, cb)


def dft_stage1(ua4, k1, tb, cb):
    bsz, s1, s2, c = ua4.shape
    return pl.pallas_call(
        _dft1_kernel,
        grid=(bsz, s2 // tb, c // cb),
        in_specs=[
            pl.BlockSpec((1, s1, tb, cb), lambda b, j, k: (b, 0, j, k)),
            pl.BlockSpec(k1.shape, lambda b, j, k: (0, 0)),
        ],
        out_specs=pl.BlockSpec((1, 2, s1, tb, cb), lambda b, j, k: (b, 0, 0, j, k)),
        out_shape=jax.ShapeDtypeStruct((bsz, 2, s1, s2, c), f32),
        compiler_params=_cp("parallel", "parallel", "parallel"),
        name="dft_stage1",
    )(ua4, k1)


def _dft2_kernel(a_ref, t_ref, cs_ref, w_ref, o_ref):
    s2, c = a_ref.shape[3], a_ref.shape[4]
    groups, ch, _ = w_ref.shape
    a = a_ref[0, :, 0].reshape(2 * s2, c).astype(bf16)
    p = _dot(t_ref[0], a)
    for g in range(groups):
        cols = slice(g * ch, (g + 1) * ch)
        pre = p[:s2, cols].astype(bf16)
        pim = p[s2:, cols].astype(bf16)
        f = _dot(pre, cs_ref[:ch, :]) + _dot(pim, cs_ref[ch:, :])
        y = _dot(f.astype(bf16), w_ref[g])
        o_ref[0, 0, :, cols] = y.astype(o_ref.dtype)


def dft_stage2(a5, t2, cs, w_a):
    bsz, _, s1, s2, c = a5.shape
    return pl.pallas_call(
        _dft2_kernel,
        grid=(bsz, s1),
        in_specs=[
            pl.BlockSpec((1, 2, 1, s2, c), lambda b, j: (b, 0, j, 0, 0)),
            pl.BlockSpec((1,) + t2.shape[1:], lambda b, j: (j, 0, 0)),
            pl.BlockSpec(cs.shape, lambda b, j: (0, 0)),
            pl.BlockSpec(w_a.shape, lambda b, j: (0, 0, 0)),
        ],
        out_specs=pl.BlockSpec((1, 1, s2, c), lambda b, j: (b, j, 0, 0)),
        out_shape=jax.ShapeDtypeStruct((bsz, s1, s2, c), bf16),
        compiler_params=_cp("parallel", "parallel"),
        name="dft_stage2",
    )(a5, t2, cs, w_a)


def _even_out_kernel(ub_ref, ubp_ref, ubn_ref, ya_ref, h_ref, wb_ref, bs_ref, wo_ref, o_ref, xs_ref,
                     *, seq):
    i = pl.program_id(1)
    nt = pl.num_programs(1)
    ts = ub_ref.shape[1]
    groups, ch, _ = wb_ref.shape
    aw = ya_ref.shape[2]
    xs_ref[0:HALO, :] = jnp.where(i > 0, ubp_ref[0].astype(f32), 0.0)
    xs_ref[HALO:HALO + ts, :] = ub_ref[0].astype(f32)
    xs_ref[HALO + ts:, :] = jnp.where(i < nt - 1, ubn_ref[0].astype(f32), 0.0)
    t = i * ts + lax.broadcasted_iota(i32, (ts, 1), 0)
    acc = h_ref[0] + _dot(ya_ref[0], wo_ref[0:aw, :])
    for g in range(groups):
        w = POOL_WINDOWS[g]
        cols = pl.ds(g * ch, ch)
        s = xs_ref[pl.ds(HALO - w // 2, ts), cols]
        for o in range(-w // 2 + 1, w // 2):
            s = s + xs_ref[pl.ds(HALO + o, ts), cols]
        cnt = (jnp.minimum(t + w // 2, seq) - jnp.maximum(t - w // 2, 0)).astype(f32)
        pooled = s / cnt - xs_ref[pl.ds(HALO, ts), cols]
        yb = _dot(pooled.astype(bf16), wb_ref[g]) * bs_ref[:, cols]
        acc = acc + _dot(yb.astype(bf16), wo_ref[pl.ds(aw + g * ch, ch), :])
    o_ref[0] = acc


def even_out(ub3, ya3, h3, w_b, b_scale, w_out, ts):
    bsz, seq, bw = ub3.shape
    d = h3.shape[2]
    aw = ya3.shape[2]
    nh = ts // HALO
    last = seq // HALO - 1
    return pl.pallas_call(
        functools.partial(_even_out_kernel, seq=seq),
        grid=(bsz, seq // ts),
        in_specs=[
            pl.BlockSpec((1, ts, bw), lambda b, i: (b, i, 0)),
            pl.BlockSpec((1, HALO, bw), lambda b, i: (b, jnp.maximum(i * nh - 1, 0), 0)),
            pl.BlockSpec((1, HALO, bw), lambda b, i: (b, jnp.minimum((i + 1) * nh, last), 0)),
            pl.BlockSpec((1, ts, aw), lambda b, i: (b, i, 0)),
            pl.BlockSpec((1, ts, d), lambda b, i: (b, i, 0)),
            pl.BlockSpec(w_b.shape, lambda b, i: (0, 0, 0)),
            pl.BlockSpec((1, bw), lambda b, i: (0, 0)),
            pl.BlockSpec(w_out.shape, lambda b, i: (0, 0)),
        ],
        out_specs=pl.BlockSpec((1, ts, d), lambda b, i: (b, i, 0)),
        out_shape=jax.ShapeDtypeStruct((bsz, seq, d), f32),
        scratch_shapes=[pltpu.VMEM((ts + 2 * HALO, bw), f32)],
        compiler_params=_cp("parallel", "parallel"),
        name="even_out",
    )(ub3, ub3, ub3, ya3, h3, w_b, b_scale, w_out)


def _odd_in_kernel(h_ref, g_ref, wa_ref, wb_ref, v_ref):
    y = _rms(h_ref[...], g_ref[...]).astype(bf16)
    a = _dot(y, wa_ref[...])
    b = _dot(y, wb_ref[...])
    v_ref[...] = a * _sigmoid(b)


def odd_in(h, g, w, tm, tn):
    n, d = h.shape
    inner = w.shape[1] // 2
    ncol = inner // tn
    return pl.pallas_call(
        _odd_in_kernel,
        grid=(ncol, n // tm),
        in_specs=[
            pl.BlockSpec((tm, d), lambda j, i: (i, 0)),
            pl.BlockSpec((1, d), lambda j, i: (0, 0)),
            pl.BlockSpec((d, tn), lambda j, i: (0, j)),
            pl.BlockSpec((d, tn), lambda j, i: (0, j + ncol)),
        ],
        out_specs=pl.BlockSpec((tm, tn), lambda j, i: (i, j)),
        out_shape=jax.ShapeDtypeStruct((n, inner), f32),
        compiler_params=_cp("parallel", "parallel"),
        name="odd_in",
    )(h, g, w, w)


CONV_ROWS = 32
CONV_COLS = 512


def _odd_out_kernel(v_ref, vp_ref, vn_ref, dw_ref, lg_ref, lb_ref, wo_ref, h_ref, o_ref, xs_ref, sh_ref,
                    cv_ref, *, taps):
    i = pl.program_id(1)
    nt = pl.num_programs(1)
    ts, inner = v_ref.shape[1], v_ref.shape[2]
    first = HALO - taps // 2
    span = sh_ref.shape[1]
    xs_ref[0:HALO, :] = jnp.where(i > 0, vp_ref[0], 0.0)
    xs_ref[HALO:HALO + ts, :] = v_ref[0]
    xs_ref[HALO + ts:, :] = jnp.where(i < nt - 1, vn_ref[0], 0.0)
    for c0 in range(0, inner, CONV_COLS):
        cols = pl.ds(c0, CONV_COLS)
        for r in range(1, SUBLANES_F32):
            sh_ref[r] = xs_ref[pl.ds(r, span), cols]
        for r0 in range(0, ts, CONV_ROWS):
            acc = jnp.zeros((CONV_ROWS, CONV_COLS), f32)
            for k in range(taps):
                q, r = divmod(first + k, SUBLANES_F32)
                rows = pl.ds(r0 + q * SUBLANES_F32, CONV_ROWS)
                x = xs_ref[rows, cols] if r == 0 else sh_ref[r, rows, :]
                acc = acc + dw_ref[k:k + 1, cols] * x
            cv_ref[pl.ds(r0, CONV_ROWS), cols] = acc
    cv = cv_ref[...]
    mu = jnp.mean(cv, axis=-1, keepdims=True)
    xc = cv - mu
    var = jnp.mean(xc * xc, axis=-1, keepdims=True)
    y = xc * lax.rsqrt(var + EPS) * lg_ref[...] + lb_ref[...]
    y = y * _sigmoid(y)
    o_ref[0] = h_ref[0] + _dot(y.astype(bf16), wo_ref[...])


def odd_out(v3, dw, ln_g, ln_b, w_out, h3, ts):
    bsz, seq, inner = v3.shape
    d = h3.shape[2]
    taps = dw.shape[0]
    nh = ts // HALO
    last = seq // HALO - 1
    return pl.pallas_call(
        functools.partial(_odd_out_kernel, taps=taps),
        grid=(bsz, seq // ts),
        in_specs=[
            pl.BlockSpec((1, ts, inner), lambda b, i: (b, i, 0)),
            pl.BlockSpec((1, HALO, inner), lambda b, i: (b, jnp.maximum(i * nh - 1, 0), 0)),
            pl.BlockSpec((1, HALO, inner), lambda b, i: (b, jnp.minimum((i + 1) * nh, last), 0)),
            pl.BlockSpec(dw.shape, lambda b, i: (0, 0)),
            pl.BlockSpec((1, inner), lambda b, i: (0, 0)),
            pl.BlockSpec((1, inner), lambda b, i: (0, 0)),
            pl.BlockSpec(w_out.shape, lambda b, i: (0, 0)),
            pl.BlockSpec((1, ts, d), lambda b, i: (b, i, 0)),
        ],
        out_specs=pl.BlockSpec((1, ts, d), lambda b, i: (b, i, 0)),
        out_shape=jax.ShapeDtypeStruct((bsz, seq, d), f32),
        scratch_shapes=[
            pltpu.VMEM((ts + 2 * HALO, inner), f32),
            pltpu.VMEM((SUBLANES_F32, ts + SUBLANES_F32 * ((HALO + taps // 2) // SUBLANES_F32), CONV_COLS), f32),
            pltpu.VMEM((ts, inner), f32),
        ],
        compiler_params=_cp("parallel", "parallel"),
        name="odd_out",
    )(v3, v3, v3, dw, ln_g, ln_b, w_out, h3)


def _router_kernel(h_ref, g_ref, wh_ref, wl_ref, b_ref, x_ref, at_ref, *, n_exp):
    tm, d = h_ref.shape
    xn = _rms(h_ref[...], g_ref[...])
    xh = xn.astype(bf16)
    xl = (xn - xh.astype(f32)).astype(bf16)
    logits = _dot(xh, wh_ref[...]) + _dot(xl, wh_ref[...]) + _dot(xh, wl_ref[...]) + b_ref[...]
    lane = lax.broadcasted_iota(i32, logits.shape, 1)
    logits = jnp.where(lane < n_exp, logits, -1e30)
    m = jnp.max(logits, axis=-1, keepdims=True)
    p = jnp.exp(logits - m)
    aff = p / jnp.sum(p, axis=-1, keepdims=True)
    x_ref[:, 0:d] = xn
    x_ref[:, d:] = aff
    at = aff.T
    for q in range(tm // ROUTE_CHUNK):
        at_ref[q] = at[0:n_exp, q * ROUTE_CHUNK:(q + 1) * ROUTE_CHUNK]


def router(h, g, wh, wl, b, n_exp, tm):
    n, d = h.shape
    return pl.pallas_call(
        functools.partial(_router_kernel, n_exp=n_exp),
        grid=(n // tm,),
        in_specs=[
            pl.BlockSpec((tm, d), lambda i: (i, 0)),
            pl.BlockSpec((1, d), lambda i: (0, 0)),
            pl.BlockSpec((d, LANES), lambda i: (0, 0)),
            pl.BlockSpec((d, LANES), lambda i: (0, 0)),
            pl.BlockSpec((1, LANES), lambda i: (0, 0)),
        ],
        out_specs=[
            pl.BlockSpec((tm, d + LANES), lambda i: (i, 0)),
            pl.BlockSpec((tm // ROUTE_CHUNK, n_exp, ROUTE_CHUNK), lambda i: (i, 0, 0)),
        ],
        out_shape=[
            jax.ShapeDtypeStruct((n, d + LANES), f32),
            jax.ShapeDtypeStruct((n // ROUTE_CHUNK, n_exp, ROUTE_CHUNK), f32),
        ],
        compiler_params=_cp("parallel"),
        name="router",
    )(h, g, wh, wl, b)


def _select_kernel(a_ref, su_ref, sl_ref, pos_ref, rank_ref, aux_ref, cst_ref, bits_ref, *, cap):
    nchunk, n_exp, cw = a_ref.shape
    bits_ref[...] = pltpu.bitcast(a_ref[...], i32)

    def count_ge(cand):
        m = (bits_ref[...] >= cand[None]).astype(f32)
        return jnp.sum(jnp.sum(m, axis=0), axis=1, keepdims=True)

    def bisect(it, prefix):
        cand = prefix | jnp.left_shift(jnp.int32(1), 30 - it)
        return jnp.where(count_ge(cand) >= cap, cand, prefix)

    thr = lax.fori_loop(0, 31, bisect, jnp.zeros((n_exp, 1), i32))
    need = cap - (count_ge(thr + 1))

    sub = lax.broadcasted_iota(i32, (SUBLANES_F32, cw), 0)

    def chunk(c, carry):
        cg, cq = carry
        b = bits_ref[c]
        g = b > thr
        q = b == thr
        gq = jnp.concatenate([g.astype(f32), q.astype(f32)], axis=0).astype(bf16)
        pre = _dot(gq, su_ref[...])
        pg = cg + pre[:n_exp]
        pq = cq + pre[n_exp:]
        sel = g | (q & (pq < need))
        pos = pg + jnp.minimum(pq, need)
        self32 = sel.astype(f32)
        pos_ref[c] = jnp.where(sel, pos, -1.0).astype(i32)
        rtok = jnp.sum(pos, axis=0, keepdims=True)
        ktok = jnp.sum(self32, axis=0, keepdims=True)
        er = _dot(sl_ref[...], self32.astype(bf16))
        rank_ref[c] = (rtok + er).astype(i32)
        cstart = cg + jnp.minimum(cq, need)
        rstart = jnp.sum(cstart, axis=0, keepdims=True)
        aux = jnp.where(sub == 0, rtok, jnp.where(sub == 1, ktok, jnp.where(sub == 2, rstart, 0.0)))
        aux_ref[c] = aux.astype(i32)
        cst_ref[c] = jnp.broadcast_to(cstart, (n_exp, LANES)).astype(i32)
        return (cg + jnp.sum(g.astype(f32), axis=1, keepdims=True),
                cq + jnp.sum(q.astype(f32), axis=1, keepdims=True))

    zero = jnp.zeros((n_exp, 1), f32)
    lax.fori_loop(0, nchunk, chunk, (zero, zero))


def select(aff3, cap):
    nchunk, n_exp, cw = aff3.shape
    r = lax.broadcasted_iota(i32, (cw, cw), 0)
    c = lax.broadcasted_iota(i32, (cw, cw), 1)
    su = (r < c).astype(bf16)
    re = lax.broadcasted_iota(i32, (n_exp, n_exp), 0)
    ce = lax.broadcasted_iota(i32, (n_exp, n_exp), 1)
    sl = (ce < re).astype(bf16)
    full3 = lambda shape: pl.BlockSpec(shape, lambda i: (0, 0, 0))
    return pl.pallas_call(
        functools.partial(_select_kernel, cap=cap),
        grid=(1,),
        in_specs=[full3(aff3.shape), pl.BlockSpec(su.shape, lambda i: (0, 0)),
                  pl.BlockSpec(sl.shape, lambda i: (0, 0))],
        out_specs=[full3((nchunk, n_exp, cw)), full3((nchunk, n_exp, cw)),
                   full3((nchunk, SUBLANES_F32, cw)), full3((nchunk, n_exp, LANES))],
        out_shape=[
            jax.ShapeDtypeStruct((nchunk, n_exp, cw), i32),
            jax.ShapeDtypeStruct((nchunk, n_exp, cw), i32),
            jax.ShapeDtypeStruct((nchunk, SUBLANES_F32, cw), i32),
            jax.ShapeDtypeStruct((nchunk, n_exp, LANES), i32),
        ],
        scratch_shapes=[pltpu.VMEM((nchunk, n_exp, cw), i32)],
        compiler_params=_cp("arbitrary"),
        name="moe_select",
    )(aff3, su, sl)


def _compact_kernel(cst_ref, pos_ref, rank_ref, idx_ref, dst_ref, acc_ref):
    e = pl.program_id(0)
    nchunk, _, cw = pos_ref.shape
    nblk = idx_ref.shape[1]
    acc_ref[...] = jnp.zeros(acc_ref.shape, f32)
    sub = lax.broadcasted_iota(i32, (SUBLANES_BF16, cw), 0)
    slot = lax.broadcasted_iota(i32, (cw, cw), 0)
    lane = lax.broadcasted_iota(i32, (1, cw), 1)

    def chunk(c, _):
        first = cst_ref[e * (nchunk + 1) + c]
        b0 = first // cw
        p = pos_ref[c, pl.ds(e, 1), :]
        rk = rank_ref[c, pl.ds(e, 1), :]
        tok = c * cw + lane
        vals = jnp.where(sub == 0, tok >> 8, jnp.where(sub == 1, tok & 255,
               jnp.where(sub == 2, rk >> 8, jnp.where(sub == 3, rk & 255, 0))))
        vt = vals.astype(f32).astype(bf16)

        for b in (b0, b0 + 1):
            onehot = (p - b * cw == slot).astype(f32).astype(bf16)
            acc_ref[b] += _dot_nt(vt, onehot)
        return 0

    lax.fori_loop(0, nchunk, chunk, 0, unroll=4)
    for b in range(nblk):
        a = acc_ref[b]
        idx_ref[0, b:b + 1, :] = (a[0:1, :] * 256.0 + a[1:2, :]).astype(i32)
        dst_ref[0, b:b + 1, :] = (a[2:3, :] * 256.0 + a[3:4, :]).astype(i32)


def compact(cst_flat, pos3, rank3, cap):
    nchunk, n_exp, cw = pos3.shape
    nblk = cap // cw
    grid_spec = pltpu.PrefetchScalarGridSpec(
        num_scalar_prefetch=1,
        grid=(n_exp,),
        in_specs=[pl.BlockSpec(pos3.shape, lambda e, s: (0, 0, 0)),
                  pl.BlockSpec(rank3.shape, lambda e, s: (0, 0, 0))],
        out_specs=[pl.BlockSpec((1, nblk, cw), lambda e, s: (e, 0, 0)),
                   pl.BlockSpec((1, nblk, cw), lambda e, s: (e, 0, 0))],
        scratch_shapes=[pltpu.VMEM((nblk + 1, SUBLANES_BF16, cw), f32)],
    )
    return pl.pallas_call(
        _compact_kernel,
        grid_spec=grid_spec,
        out_shape=[jax.ShapeDtypeStruct((n_exp, nblk, cw), i32),
                   jax.ShapeDtypeStruct((n_exp, nblk, cw), i32)],
        compiler_params=_cp("arbitrary"),
        name="moe_compact",
    )(cst_flat, pos3, rank3)


def _expert_kernel(idx_ref, dst_ref, x_hbm, wg_ref, wu_ref, wd_ref, z_hbm,
                   xbuf, xb16, vcol, ybuf, gsem, ssem, *, nf_static):
    e = pl.program_id(0)
    m = pl.program_id(1)
    f = pl.program_id(2)
    nm = pl.num_programs(1)
    nf = pl.num_programs(2)
    _, bm, d = ybuf.shape
    blk = e * nm + m
    last_blk = pl.num_programs(0) * nm - 1
    s = blk % 2
    rows = bm // nf_static

    def gather_row(b, j, slot):
        return pltpu.make_async_copy(x_hbm.at[pl.ds(idx_ref[b * bm + j], 1), :],
                                     xbuf.at[slot, pl.ds(j, 1), :], gsem.at[slot])

    def scatter_row(b, j, slot):
        return pltpu.make_async_copy(ybuf.at[slot, pl.ds(j, 1), :],
                                     z_hbm.at[pl.ds(dst_ref[b * bm + j], 1), :], ssem.at[slot])

    def wait_gathers(slot):
        pltpu.make_async_copy(x_hbm.at[pl.ds(0, bm), :], xbuf.at[slot], gsem.at[slot]).wait()

    def wait_scatters(slot):
        pltpu.make_async_copy(ybuf.at[slot], z_hbm.at[pl.ds(0, bm), :], ssem.at[slot]).wait()

    @pl.when((blk == 0) & (f == 0))
    def _():
        ybuf[1] = jnp.zeros((bm, d), f32)

        def start(j, _):
            gather_row(0, j, 0).start()
            return 0
        lax.fori_loop(0, bm, start, 0, unroll=ISSUE_UNROLL)

    @pl.when(f == 0)
    def _():
        wait_gathers(s)

        @pl.when(blk > 0)
        def _():
            wait_scatters(s)

        xb16[...] = xbuf[s, :, 0:d].astype(bf16)
        lane = lax.broadcasted_iota(i32, (bm, LANES), 1)
        vcol[...] = jnp.sum(jnp.where(lane == e, xbuf[s, :, d:], 0.0), axis=1, keepdims=True)

    nxt = jnp.minimum(blk + 1, last_blk)
    prv = jnp.maximum(blk - 1, 0)
    for step in range(nf_static):
        @pl.when(f == step)
        def _():
            for j in range(step * rows, (step + 1) * rows):
                gather_row(nxt, j, 1 - s).start()
                scatter_row(prv, j, 1 - s).start()

    x = xb16[...]
    gate = _dot(x, wg_ref[...])
    up = _dot(x, wu_ref[...])
    hid = (gate * _sigmoid(gate) * up).astype(bf16)

    if nf_static == 1:
        ybuf[s] = _dot(hid, wd_ref[...]) * vcol[...]
    else:
        @pl.when(f == 0)
        def _():
            ybuf[s] = _dot(hid, wd_ref[...])

        @pl.when((f > 0) & (f < nf - 1))
        def _():
            ybuf[s] += _dot(hid, wd_ref[...])

        @pl.when(f == nf - 1)
        def _():
            ybuf[s] = (ybuf[s] + _dot(hid, wd_ref[...])) * vcol[...]

    @pl.when((blk == last_blk) & (f == nf - 1))
    def _():
        wait_gathers(1 - s)
        wait_scatters(1 - s)

        def start(j, _):
            scatter_row(blk, j, s).start()
            return 0
        lax.fori_loop(0, bm, start, 0, unroll=ISSUE_UNROLL)
        wait_scatters(s)


def experts(idx, dst, xext, wg, wu, wd, layer, cap, bm, tf):
    _, n_exp, d, hidden = wg.shape
    grid_spec = pltpu.PrefetchScalarGridSpec(
        num_scalar_prefetch=2,
        grid=(n_exp, cap // bm, hidden // tf),
        in_specs=[
            pl.BlockSpec(memory_space=pl.ANY),
            pl.BlockSpec((None, None, d, tf), lambda e, m, f, i_, d_: (layer, e, 0, f)),
            pl.BlockSpec((None, None, d, tf), lambda e, m, f, i_, d_: (layer, e, 0, f)),
            pl.BlockSpec((None, None, tf, d), lambda e, m, f, i_, d_: (layer, e, f, 0)),
        ],
        out_specs=pl.BlockSpec(memory_space=pl.ANY),
        scratch_shapes=[
            pltpu.VMEM((2, bm, d + LANES), f32),
            pltpu.VMEM((bm, d), bf16),
            pltpu.VMEM((bm, 1), f32),
            pltpu.VMEM((2, bm, d), f32),
            pltpu.SemaphoreType.DMA((2,)),
            pltpu.SemaphoreType.DMA((2,)),
        ],
    )
    assert cap % bm == 0 and bm % (hidden // tf) == 0
    return pl.pallas_call(
        functools.partial(_expert_kernel, nf_static=hidden // tf),
        grid_spec=grid_spec,
        out_shape=jax.ShapeDtypeStruct((n_exp * cap, d), f32),
        compiler_params=pltpu.CompilerParams(dimension_semantics=("arbitrary",) * 3,
                                             vmem_limit_bytes=EXPERT_VMEM_LIMIT_BYTES),
        name="moe_experts",
    )(idx, dst, xext, wg, wu, wd)


def _combine_kernel(ts_ref, h_ref, aux_ref, eye_ref, z_hbm, p_ref, g_ref, wg_ref, wp_ref, gf_ref, o_ref,
                    zbuf, sem, slot_ref, *, total, final):
    i = pl.program_id(0)
    nt = pl.num_programs(0)
    tm, d = h_ref.shape
    cw = zbuf.shape[1]
    cpt = tm // cw
    sub = lax.broadcasted_iota(i32, (SUBLANES_BF16, cw), 0)
    cols = []
    for q in range(cpt):
        rt = aux_ref[q, 0:1, :]
        kt = aux_ref[q, 1:2, :]
        parts = jnp.where(sub == 0, rt >> 8, jnp.where(sub == 1, rt & 255, jnp.where(sub == 2, kt, 0)))
        cols.append(_dot_nt(eye_ref[...], parts.astype(f32).astype(bf16)))
    col = jnp.concatenate(cols, axis=0) if cpt > 1 else cols[0]
    rlo = col[:, 0:1] * 256.0 + col[:, 1:2]
    rhi = rlo + col[:, 2:3]

    def first_row(t):
        return (ts_ref[t * cpt] // SUBLANES_F32) * SUBLANES_F32

    def window(nominal):
        return pl.multiple_of(jnp.minimum(nominal, total - cw), SUBLANES_F32)

    def chunk_copy(nominal, slot):
        return pltpu.make_async_copy(z_hbm.at[pl.ds(window(nominal), cw), :], zbuf.at[slot], sem.at[slot])

    a0 = first_row(i)
    nch = jnp.maximum((ts_ref[(i + 1) * cpt] - a0 + cw - 1) // cw, 1)

    @pl.when(i == 0)
    def _():
        slot_ref[0] = 0
        chunk_copy(a0, 0).start()

    s0 = slot_ref[0]
    o_ref[...] = h_ref[...]
    lane = lax.broadcasted_iota(i32, (1, cw), 1)

    def chunk(k, _):
        slot = (s0 + k) % 2
        nominal = a0 + k * cw

        @pl.when(k + 1 < nch)
        def _():
            chunk_copy(nominal + cw, 1 - slot).start()

        @pl.when((k + 1 == nch) & (i + 1 < nt))
        def _():
            chunk_copy(first_row(i + 1), 1 - slot).start()
            slot_ref[0] = 1 - slot

        chunk_copy(nominal, slot).wait()
        first = window(nominal)
        row = first + lane
        rowf = row.astype(f32)
        zb = zbuf[slot].astype(bf16)
        for q in range(cpt):
            @pl.when((ts_ref[i * cpt + q + 1] > nominal) & (ts_ref[i * cpt + q] < first + cw))
            def _():
                tok = slice(q * cw, (q + 1) * cw)
                onehot = ((rowf >= rlo[tok]) & (rowf < rhi[tok]) & (row >= nominal)).astype(f32).astype(bf16)
                o_ref[tok, :] += _dot(onehot, zb)
        return 0

    lax.fori_loop(0, nch, chunk, 0)

    hc = o_ref[...]
    gate = _sigmoid(_dot(_rms(hc, g_ref[...]).astype(bf16), wg_ref[...]))
    out = hc + gate * _dot(p_ref[...].astype(bf16), wp_ref[...])
    if final:
        out = _rms(out, gf_ref[...])
    o_ref[...] = out


def combine_ple(tstart, h, aux3, z, p, g, w_gate, w_proj, g_final, final, tm):
    n, d = h.shape
    cw = aux3.shape[2]
    pd = p.shape[1]
    total = z.shape[0]
    eye = jnp.eye(cw, dtype=bf16)
    grid_spec = pltpu.PrefetchScalarGridSpec(
        num_scalar_prefetch=1,
        grid=(n // tm,),
        in_specs=[
            pl.BlockSpec((tm, d), lambda i, s: (i, 0)),
            pl.BlockSpec((tm // cw,) + aux3.shape[1:], lambda i, s: (i, 0, 0)),
            pl.BlockSpec(eye.shape, lambda i, s: (0, 0)),
            pl.BlockSpec(memory_space=pl.ANY),
            pl.BlockSpec((tm, pd), lambda i, s: (i, 0)),
            pl.BlockSpec((1, d), lambda i, s: (0, 0)),
            pl.BlockSpec((d, d), lambda i, s: (0, 0)),
            pl.BlockSpec((pd, d), lambda i, s: (0, 0)),
            pl.BlockSpec((1, d), lambda i, s: (0, 0)),
        ],
        out_specs=pl.BlockSpec((tm, d), lambda i, s: (i, 0)),
        scratch_shapes=[pltpu.VMEM((2, cw, d), f32), pltpu.SemaphoreType.DMA((2,)), pltpu.SMEM((1,), i32)],
    )
    return pl.pallas_call(
        functools.partial(_combine_kernel, total=total, final=final),
        grid_spec=grid_spec,
        out_shape=jax.ShapeDtypeStruct((n, d), f32),
        compiler_params=_cp("arbitrary"),
        name="combine_ple_final" if final else "combine_ple",
    )(tstart, h, aux3, eye, z, p, g, w_gate, w_proj, g_final)


def moe_ple(h, g, wr_hi, wr_lo, b_r, wg, wu, wd, layer, n_exp, tm, bm, tf, p, g_ple, w_pg, w_pp, g_final,
            final):
    n, d = h.shape
    cap = CAPACITY_FACTOR * n // n_exp
    xext, aff3 = router(h, g, wr_hi, wr_lo, b_r, n_exp, tm)
    pos3, rank3, aux3, cst3 = select(aff3, cap)
    cst_flat = jnp.concatenate([cst3[:, :, 0].T, jnp.full((n_exp, 1), cap, i32)], axis=1).reshape(-1)
    idx, dst = compact(cst_flat, pos3, rank3, cap)
    z = experts(idx.reshape(-1), dst.reshape(-1), xext, wg, wu, wd, layer, cap, bm, tf)
    tstart = jnp.concatenate([aux3[:, 2, 0], jnp.full((1,), n_exp * cap, i32)])
    return combine_ple(tstart, h, aux3, z, p, g_ple, w_pg, w_pp, g_final, final, tm)


def _tiles(n, seq, cap):
    tm = min(512, n)
    ts = min(256, seq)
    bm = min(512, cap)
    return tm, ts, bm


def _run_trunk(x, p, wts, n_exp):
    bsz, seq, d = x.shape
    depth = p.shape[0]
    n = bsz * seq
    cap = CAPACITY_FACTOR * n // n_exp
    tm, ts, bm = _tiles(n, seq, cap)
    hidden = wts["w_gate"].shape[-1]
    tf = min(1024, hidden)
    inner = wts["w_c_out"].shape[1]
    tn = min(1024, inner)
    aw = wts["w_a_mix"].shape[1] * wts["w_a_mix"].shape[2]
    ch = wts["w_a_mix"].shape[2]
    s2 = DFT_INNER
    s1 = seq // s2
    tb = min(s2, max(SUBLANES_F32, DFT_ROWS // s1))
    k1, t2, cs = _dft_tables(seq, tb, ch)
    cb = min(512, aw)

    h = x.reshape(n, d)
    for i in range(depth):
        j = i // 2
        if i % 2 == 0:
            ua, ub = even_in(h, wts["g_mix"][i], wts["w_ab_in"][j], tm)
            a5 = dft_stage1(ua.reshape(bsz, s1, s2, aw), k1, tb, cb)
            yp = dft_stage2(a5, t2, cs, wts["w_a_mix"][j])
            ya = jnp.swapaxes(yp, 1, 2).reshape(bsz, seq, aw)
            h = even_out(ub.reshape(bsz, seq, -1), ya, h.reshape(bsz, seq, d), wts["w_b_mix"][j],
                         wts["b_scale"][j], wts["w_ab_out"][j], ts).reshape(n, d)
        else:
            v = odd_in(h, wts["g_mix"][i], wts["w_c_in"][j], tm, tn)
            h = odd_out(v.reshape(bsz, seq, inner), wts["c_dw"][j], wts["c_ln_g"][j], wts["c_ln_b"][j],
                        wts["w_c_out"][j], h.reshape(bsz, seq, d), ts).reshape(n, d)
        h = moe_ple(h, wts["g_ffn"][i], wts["w_router_hi"][i], wts["w_router_lo"][i], wts["b_router"][i],
                    wts["w_gate"], wts["w_up"], wts["w_down"], i, n_exp, tm, bm, tf,
                    p[i].reshape(n, -1), wts["g_ple"][i], wts["w_ple_gate"][i], wts["w_ple_proj"][i],
                    wts["g_final"], i == depth - 1)
    return h.reshape(bsz, seq, d)


def kernel(x_prompt, x_sample, p_prompt, p_sample, g_mix, w_ab_in, w_a_mix, w_b_mix, b_scale, w_ab_out,
           w_c_in, c_dw, c_ln_g, c_ln_b, w_c_out, g_ffn, w_router, b_router, w_gate, w_up, w_down,
           g_ple, w_ple_gate, w_ple_proj, g_final):
    depth, d = g_mix.shape
    n_exp = w_router.shape[-1]
    wr = jnp.pad(w_router, ((0, 0), (0, 0), (0, LANES - n_exp)))
    wr_hi = wr.astype(bf16)
    wr_lo = (wr - wr_hi.astype(f32)).astype(bf16)
    wts = dict(
        g_mix=g_mix[:, None, :],
        w_ab_in=w_ab_in.astype(bf16),
        w_a_mix=w_a_mix.astype(bf16),
        w_b_mix=w_b_mix.astype(bf16),
        b_scale=b_scale[:, None, :],
        w_ab_out=w_ab_out.astype(bf16),
        w_c_in=w_c_in.astype(bf16),
        c_dw=c_dw,
        c_ln_g=c_ln_g[:, None, :],
        c_ln_b=c_ln_b[:, None, :],
        w_c_out=w_c_out.astype(bf16),
        g_ffn=g_ffn[:, None, :],
        w_router_hi=wr_hi,
        w_router_lo=wr_lo,
        b_router=jnp.pad(b_router, ((0, 0), (0, LANES - n_exp)))[:, None, :],
        w_gate=w_gate.astype(bf16),
        w_up=w_up.astype(bf16),
        w_down=w_down.astype(bf16),
        g_ple=g_ple[:, None, :],
        w_ple_gate=w_ple_gate.astype(bf16),
        w_ple_proj=w_ple_proj.astype(bf16),
        g_final=g_final[None, :],
    )
    y_prompt = _run_trunk(x_prompt, p_prompt, wts, n_exp)
    y_sample = _run_trunk(x_sample, p_sample, wts, n_exp)
    return (y_prompt, y_sample)
```
